```python
import jax, jax.numpy as jnp
from jax import lax
import numpy as np

D_MODEL = 1024
BATCH = 8
SEQ = 8192
DEPTH = 1
DEC_BATCH = 8
DEC_SEQ = 64
PAST_LEN = 1024

CHUNK = 64
Q_BLOCK = 128
MLA_HEADS = 8
MLA_NOPE = 64
MLA_ROPE = 32
MLA_QK = MLA_NOPE + MLA_ROPE
MLA_V = 64
MLA_Q_RANK = 384
MLA_KV_RANK = 256
ROPE_THETA = 10000.0
GLA_HEADS = 4
GLA_DK = 64
GLA_DV = 128
GLA_GATE_RANK = 16
GLA_TAU = 16.0
MIX_WIDTH = MLA_HEADS * MLA_V + GLA_HEADS * GLA_DV
FFN_DIM = 2816
CONV_W = 3
EPS = 1e-6
COL_SIZES = (MLA_Q_RANK, MLA_KV_RANK, MLA_ROPE, GLA_HEADS * GLA_DK, GLA_HEADS * GLA_DK,
             GLA_HEADS * GLA_DV, GLA_GATE_RANK, GLA_HEADS * GLA_DV)
IN_COLS = (MLA_Q_RANK + MLA_KV_RANK + MLA_ROPE + 2 * GLA_HEADS * GLA_DK
           + GLA_HEADS * GLA_DV + GLA_GATE_RANK + GLA_HEADS * GLA_DV)

kernel_name = 'hybrid_mla_gla_convffn_stream_step'


def _rms(x, g):
    xf = x.astype(jnp.float32)
    y = xf * lax.rsqrt(jnp.mean(xf * xf, axis=-1, keepdims=True) + EPS)
    return (y * g.astype(jnp.float32)).astype(x.dtype)


def _rope(x, pos):
    half = MLA_ROPE // 2
    inv = 1.0 / (ROPE_THETA ** (jnp.arange(half, dtype=jnp.float32) / half))
    ang = pos.astype(jnp.float32)[:, None] * inv[None, :]
    cos = jnp.cos(ang)[None, :, None, :]
    sin = jnp.sin(ang)[None, :, None, :]
    xr = x[..., MLA_NOPE:].astype(jnp.float32)
    x1, x2 = xr[..., :half], xr[..., half:]
    rot = jnp.concatenate([x1 * cos - x2 * sin, x2 * cos + x1 * sin], axis=-1).astype(x.dtype)
    return jnp.concatenate([x[..., :MLA_NOPE], rot], axis=-1)


def _split_cols(p):
    out, o = [], 0
    for n in COL_SIZES:
        out.append(p[..., o:o + n])
        o += n
    return out


def _adaln(c, w_ada, b_ada):
    return jnp.split(jax.nn.silu(c) @ w_ada + b_ada, 6, axis=-1)


def _modulate(x, g, shift, scale):
    return _rms(x, g) * (1.0 + scale[:, None, :]) + shift[:, None, :]


def _mla_queries(q_lat, pos, g_qa, w_uq, g_qn):
    B, L, _ = q_lat.shape
    q = (_rms(q_lat, g_qa) @ w_uq).reshape(B, L, MLA_HEADS, MLA_QK)
    return _rope(_rms(q, g_qn), pos)


def _mla_keys(ckv, kpe, pos, w_ukv, g_kn):
    B, L, _ = ckv.shape
    kv = (ckv @ w_ukv).reshape(B, L, MLA_HEADS, MLA_NOPE + MLA_V)
    k_nope, v = kv[..., :MLA_NOPE], kv[..., MLA_NOPE:]
    k_pe = jnp.broadcast_to(kpe[:, :, None, :], (B, L, MLA_HEADS, MLA_ROPE)).astype(k_nope.dtype)
    k = jnp.concatenate([k_nope, k_pe], axis=-1)
    return _rope(_rms(k, g_kn), pos), v


def _attend(q, k, v, q_pos, k_pos):
    s = jnp.einsum('bqhd,bkhd->bhqk', q, k, preferred_element_type=jnp.float32) * (MLA_QK ** -0.5)
    visible = (k_pos[None, :] // CHUNK) <= (q_pos[:, None] // CHUNK)
    s = jnp.where(visible[None, None], s, jnp.finfo(jnp.float32).min)
    p = jax.nn.softmax(s, axis=-1).astype(v.dtype)
    return jnp.einsum('bhqk,bkhd->bqhd', p, v)


def _gla_log_forget(g_lr, w_a2, b_a2):
    z = (g_lr @ w_a2 + b_a2).astype(jnp.float32)
    return jax.nn.log_sigmoid(z) / GLA_TAU


def _gla_chunk(q, k, v, lg, s0):
    q, k, v = (t.astype(jnp.float32) for t in (q, k, v))
    L = q.shape[2]
    b = jnp.cumsum(lg, axis=2)
    causal = jnp.tril(jnp.ones((L, L), dtype=bool))
    diff = b[:, :, :, None, :] - b[:, :, None, :, :]
    decay = jnp.exp(jnp.where(causal[None, None, :, :, None], diff, -jnp.inf))
    a = jnp.einsum('bhid,bhjd,bhijd->bhij', q, k, decay)
    o = (jnp.einsum('bhij,bhje->bhie', a, v)
         + jnp.einsum('bhid,bhde->bhie', q * jnp.exp(b), s0))
    b_last = b[:, :, -1, :]
    s = (jnp.exp(b_last)[..., None] * s0
         + jnp.einsum('bhjd,bhje->bhde', k * jnp.exp(b_last[:, :, None, :] - b), v))
    return o, s


def _conv_ffn(h, hist, p):
    L = h.shape[1]
    a, g = jnp.split(h @ p['w_up'], 2, axis=-1)
    a_ext = jnp.concatenate([hist.astype(a.dtype), a], axis=1)
    conv = p['b_conv']
    for j in range(CONV_W):
        conv = conv + p['w_conv'][j] * a_ext[:, j:j + L]
    y = (jax.nn.gelu(conv) * g) @ p['w_down']
    return y, a_ext[:, L:]


def _mixer_inputs(x, c, p):
    B, L, _ = x.shape
    mods = _adaln(c, p['w_ada'], p['b_ada'])
    h = _modulate(x, p['g_norm1'], mods[0], mods[1])
    q_lat, kv_lat, kpe, gq, gk, gv, g_lr, og = _split_cols(h @ p['w_in'])
    ckv = _rms(kv_lat, p['g_kva'])
    gla = (gq.reshape(B, L, GLA_HEADS, GLA_DK) * (GLA_DK ** -0.5),
           gk.reshape(B, L, GLA_HEADS, GLA_DK),
           gv.reshape(B, L, GLA_HEADS, GLA_DV),
           _gla_log_forget(g_lr, p['w_a2'], p['b_a2']).reshape(B, L, GLA_HEADS, GLA_DK))
    return mods, q_lat, ckv, kpe, gla, og


def _layer_out(x, o_mla, o_gla, og, mods, conv_hist, p):
    B, L, _ = x.shape
    o_gla = _rms(o_gla.astype(x.dtype), p['g_gla']) * jax.nn.silu(og.reshape(B, L, GLA_HEADS, GLA_DV))
    mixed = jnp.concatenate([o_mla.reshape(B, L, MLA_HEADS * MLA_V),
                             o_gla.reshape(B, L, GLA_HEADS * GLA_DV)], axis=-1) @ p['w_out']
    x = x + mods[2][:, None, :] * mixed
    h = _modulate(x, p['g_norm2'], mods[3], mods[4])
    f, new_hist = _conv_ffn(h, conv_hist, p)
    return x + mods[5][:, None, :] * f, new_hist


def _layer_prompt(x, c, p):
    B, L, _ = x.shape
    pos = jnp.arange(L, dtype=jnp.int32)
    mods, q_lat, ckv, kpe, (gq, gk, gv, lg), og = _mixer_inputs(x, c, p)
    q = _mla_queries(q_lat, pos, p['g_qa'], p['w_uq'], p['g_qn'])
    k, v = _mla_keys(ckv, kpe, pos, p['w_ukv'], p['g_kn'])
    nb = L // Q_BLOCK
    q_blocks = q.reshape(B, nb, Q_BLOCK, MLA_HEADS, MLA_QK).swapaxes(0, 1)
    pos_blocks = pos.reshape(nb, Q_BLOCK)
    o_mla = lax.map(lambda blk: _attend(blk[0], k, v, blk[1], pos), (q_blocks, pos_blocks))
    o_mla = o_mla.swapaxes(0, 1).reshape(B, L, MLA_HEADS, MLA_V)
    nc = L // CHUNK
    to_blocks = lambda t: t.reshape(B, nc, CHUNK, GLA_HEADS, t.shape[-1]).transpose(1, 0, 3, 2, 4)
    s0 = jnp.zeros((B, GLA_HEADS, GLA_DK, GLA_DV), jnp.float32)

    def step(s, blk):
        o_b, s_new = _gla_chunk(blk[0], blk[1], blk[2], blk[3], s)
        return s_new, o_b

    s_fin, o_gla = lax.scan(step, s0, (to_blocks(gq), to_blocks(gk), to_blocks(gv), to_blocks(lg)))
    o_gla = o_gla.transpose(1, 0, 3, 2, 4).reshape(B, L, GLA_HEADS, GLA_DV)
    hist0 = jnp.zeros((B, CONV_W - 1, FFN_DIM), x.dtype)
    y, new_hist = _layer_out(x, o_mla, o_gla, og, mods, hist0, p)
    return y, ckv, kpe, s_fin, new_hist


def _layer_sample(x, c, cache_ckv, cache_kpe, s_gla, conv_hist, p):
    B, T, _ = x.shape
    P = cache_ckv.shape[1]
    q_pos = P + jnp.arange(T, dtype=jnp.int32)
    k_pos = jnp.arange(P + T, dtype=jnp.int32)
    mods, q_lat, ckv, kpe, (gq, gk, gv, lg), og = _mixer_inputs(x, c, p)
    q = _mla_queries(q_lat, q_pos, p['g_qa'], p['w_uq'], p['g_qn'])
    ckv_all = jnp.concatenate([cache_ckv.astype(ckv.dtype), ckv], axis=1)
    kpe_all = jnp.concatenate([cache_kpe.astype(kpe.dtype), kpe], axis=1)
    k, v = _mla_keys(ckv_all, kpe_all, k_pos, p['w_ukv'], p['g_kn'])
    o_mla = _attend(q, k, v, q_pos, k_pos)
    hd = lambda t: t.transpose(0, 2, 1, 3)
    o_gla, s_new = _gla_chunk(hd(gq), hd(gk), hd(gv), hd(lg), s_gla.astype(jnp.float32))
    o_gla = o_gla.transpose(0, 2, 1, 3)
    y, new_hist = _layer_out(x, o_mla, o_gla, og, mods, conv_hist, p)
    return y, ckv, kpe, s_new, new_hist


def setup_inputs(seed: int = 0) -> dict:
    key = jax.random.key(seed)
    ks = list(jax.random.split(key, 32))

    def nrm(i, shape, s=1.0):
        return jax.random.normal(ks[i], shape, jnp.float32) * s

    def gain(i, n):
        return 1.0 + nrm(i, (DEPTH, n), 0.02)

    return {
        'x_prompt': nrm(0, (BATCH, SEQ, D_MODEL)),
        'x_sample': nrm(1, (DEC_BATCH, DEC_SEQ, D_MODEL)),
        'c_prompt': nrm(2, (BATCH, D_MODEL)),
        'c_sample': nrm(3, (DEC_BATCH, D_MODEL)),
        'cache_ckv': nrm(4, (DEPTH, DEC_BATCH, PAST_LEN, MLA_KV_RANK)),
        'cache_kpe': nrm(5, (DEPTH, DEC_BATCH, PAST_LEN, MLA_ROPE)),
        'state_gla': nrm(6, (DEPTH, DEC_BATCH, GLA_HEADS, GLA_DK, GLA_DV)),
        'state_ffn_conv': nrm(7, (DEPTH, DEC_BATCH, CONV_W - 1, FFN_DIM)),
        'w_ada': nrm(8, (DEPTH, D_MODEL, 6 * D_MODEL), 0.5 * D_MODEL ** -0.5),
        'b_ada': nrm(9, (DEPTH, 6 * D_MODEL), 0.02),
        'g_norm1': gain(10, D_MODEL),
        'w_in': nrm(11, (DEPTH, D_MODEL, IN_COLS), D_MODEL ** -0.5),
        'g_qa': gain(12, MLA_Q_RANK),
        'w_uq': nrm(13, (DEPTH, MLA_Q_RANK, MLA_HEADS * MLA_QK), MLA_Q_RANK ** -0.5),
        'g_qn': gain(14, MLA_QK),
        'g_kva': gain(15, MLA_KV_RANK),
        'w_ukv': nrm(16, (DEPTH, MLA_KV_RANK, MLA_HEADS * (MLA_NOPE + MLA_V)), MLA_KV_RANK ** -0.5),
        'g_kn': gain(17, MLA_QK),
        'w_a2': nrm(18, (DEPTH, GLA_GATE_RANK, GLA_HEADS * GLA_DK), GLA_GATE_RANK ** -0.5),
        'b_a2': nrm(19, (DEPTH, GLA_HEADS * GLA_DK), 0.1),
        'g_gla': gain(20, GLA_DV),
        'w_out': nrm(21, (DEPTH, MIX_WIDTH, D_MODEL), MIX_WIDTH ** -0.5),
        'g_norm2': gain(22, D_MODEL),
        'w_up': nrm(23, (DEPTH, D_MODEL, 2 * FFN_DIM), D_MODEL ** -0.5),
        'w_conv': nrm(24, (DEPTH, CONV_W, FFN_DIM), CONV_W ** -0.5),
        'b_conv': nrm(25, (DEPTH, FFN_DIM), 0.02),
        'w_down': nrm(26, (DEPTH, FFN_DIM, D_MODEL), FFN_DIM ** -0.5),
    }


def reference(x_prompt, x_sample, c_prompt, c_sample, cache_ckv, cache_kpe, state_gla, state_ffn_conv,
              w_ada, b_ada, g_norm1, w_in, g_qa, w_uq, g_qn, g_kva, w_ukv, g_kn, w_a2, b_a2, g_gla,
              w_out, g_norm2, w_up, w_conv, b_conv, w_down):
    yp, ys = x_prompt, x_sample
    ckv_p, kpe_p, gla_p, conv_p = [], [], [], []
    ckv_s, kpe_s, gla_s, conv_s = [], [], [], []
    for l in range(DEPTH):
        p = {'w_ada': w_ada[l], 'b_ada': b_ada[l], 'g_norm1': g_norm1[l], 'w_in': w_in[l],
             'g_qa': g_qa[l], 'w_uq': w_uq[l], 'g_qn': g_qn[l], 'g_kva': g_kva[l],
             'w_ukv': w_ukv[l], 'g_kn': g_kn[l], 'w_a2': w_a2[l], 'b_a2': b_a2[l],
             'g_gla': g_gla[l], 'w_out': w_out[l], 'g_norm2': g_norm2[l], 'w_up': w_up[l],
             'w_conv': w_conv[l], 'b_conv': b_conv[l], 'w_down': w_down[l]}
        yp, a, b, s, h = _layer_prompt(yp, c_prompt, p)
        ckv_p.append(a); kpe_p.append(b); gla_p.append(s); conv_p.append(h)
        ys, a, b, s, h = _layer_sample(ys, c_sample, cache_ckv[l], cache_kpe[l], state_gla[l],
                                       state_ffn_conv[l], p)
        ckv_s.append(a); kpe_s.append(b); gla_s.append(s); conv_s.append(h)
    return (yp, ys, jnp.stack(ckv_p), jnp.stack(kpe_p), jnp.stack(gla_p), jnp.stack(conv_p),
            jnp.stack(ckv_s), jnp.stack(kpe_s), jnp.stack(gla_s), jnp.stack(conv_s))
```

```python
import functools
import math

import jax
import jax.numpy as jnp
from jax import lax
from jax.experimental import pallas as pl
from jax.experimental.pallas import tpu as pltpu

F32 = jnp.float32
BF16 = jnp.bfloat16

D_MODEL = 1024
CHUNK = 64
CHUNK_SHIFT = 6
MLA_HEADS = 8
MLA_NOPE = 64
MLA_ROPE = 32
MLA_QK = MLA_NOPE + MLA_ROPE
MLA_V = 64
MLA_Q_RANK = 384
MLA_KV_RANK = 256
ROPE_THETA = 10000.0
GLA_HEADS = 4
GLA_DK = 64
GLA_DV = 128
GLA_GATE_RANK = 16
GLA_TAU = 16.0
FFN_DIM = 2816
CONV_W = 3
EPS = 1e-6

LANES = 128
HEAD_PAD = LANES
HALF_ROPE = MLA_ROPE // 2
GLA_K_ALL = GLA_HEADS * GLA_DK
GLA_V_ALL = GLA_HEADS * GLA_DV
C_QLAT = 0
C_KV = C_QLAT + MLA_Q_RANK
C_GQ = C_KV + MLA_KV_RANK
C_GK = C_GQ + GLA_K_ALL
C_GV = C_GK + GLA_K_ALL
C_OG = C_GV + GLA_V_ALL
C_MISC = C_OG + GLA_V_ALL
IN_COLS_PAD = C_MISC + LANES
FFN_TILE = 256
EXP_CLAMP = 80.0
VMEM_LIMIT = 56 * 1024 * 1024


def _const_spec(shape):
    nd = len(shape)
    return pl.BlockSpec(shape, lambda *_: (0,) * nd, pipeline_mode=pl.Buffered(1))


def _params(*sem):
    return pltpu.CompilerParams(dimension_semantics=sem, vmem_limit_bytes=VMEM_LIMIT)


def _rms_rows(x):
    return x * lax.rsqrt(jnp.mean(x * x, axis=-1, keepdims=True) + EPS)


def _ada_kernel(c_ref, w_ref, b_ref, o_ref):
    c = c_ref[...]
    s = c * jax.nn.sigmoid(c)
    o_ref[...] = jnp.dot(s, w_ref[...], precision=lax.Precision.HIGHEST,
                         preferred_element_type=F32) + b_ref[...]


def _adaln(c, w_ada, b_ada):
    n, d = c.shape
    cols = w_ada.shape[1]
    tn = 1024
    return pl.pallas_call(
        _ada_kernel,
        grid=(cols // tn,),
        in_specs=[pl.BlockSpec((n, d), lambda j: (0, 0)),
                  pl.BlockSpec((d, tn), lambda j: (0, j)),
                  pl.BlockSpec((1, tn), lambda j: (0, j))],
        out_specs=pl.BlockSpec((n, tn), lambda j: (0, j)),
        out_shape=jax.ShapeDtypeStruct((n, cols), F32),
        compiler_params=_params("arbitrary"),
        name="adaln",
    )(c, w_ada, b_ada.reshape(1, cols))


def _inproj_kernel(x_ref, mods_ref, g1_ref, win_ref, gqa_ref, wuq_ref, gq3_ref, cos_ref, sin_ref,
                   gkva_ref, wa2_ref, ba2_ref,
                   ckv_ref, kpe_ref, q_ref, gq_ref, gk_ref, gv_ref, lg_ref, og_ref):
    x = x_ref[0]
    shift = mods_ref[0, 0:1, :]
    scale = mods_ref[0, 1:2, :]
    h = _rms_rows(x) * g1_ref[...] * (1.0 + scale) + shift
    p = jnp.dot(h.astype(BF16), win_ref[...], preferred_element_type=F32)

    ckv_ref[0] = _rms_rows(p[:, C_KV:C_GQ]) * gkva_ref[...]
    misc = p[:, C_MISC:IN_COLS_PAD]
    kpe_ref[0] = misc[:, 0:MLA_ROPE]

    qa = _rms_rows(p[:, C_QLAT:C_KV]) * gqa_ref[...]
    qu = jnp.dot(qa.astype(BF16), wuq_ref[...], preferred_element_type=F32)
    tq = gq3_ref[0:1, :] + cos_ref[...] * gq3_ref[1:2, :] + sin_ref[...] * gq3_ref[2:3, :]
    lane = lax.broadcasted_iota(jnp.int32, (1, HEAD_PAD), 1)
    real = (lane < MLA_QK).astype(F32)
    for hd in range(MLA_HEADS):
        xh = qu[:, hd * HEAD_PAD:(hd + 1) * HEAD_PAD]
        ss = jnp.sum(xh * xh * real, axis=-1, keepdims=True)
        r = lax.rsqrt(ss * (1.0 / MLA_QK) + EPS)
        q_ref[0, :, hd * HEAD_PAD:(hd + 1) * HEAD_PAD] = (xh * r * tq).astype(BF16)

    gq_ref[0] = (p[:, C_GQ:C_GK] * (GLA_DK ** -0.5)).astype(BF16)
    gk_ref[0] = p[:, C_GK:C_GV].astype(BF16)
    gv_ref[0] = p[:, C_GV:C_OG].astype(BF16)
    og_ref[0] = p[:, C_OG:C_MISC].astype(BF16)
    z = jnp.dot(misc.astype(BF16), wa2_ref[...], preferred_element_type=F32) + ba2_ref[...]
    log_sig = jnp.minimum(z, 0.0) - jnp.log1p(jnp.exp(-jnp.abs(z)))
    lg_ref[0] = log_sig * (1.0 / GLA_TAU)


def _inproj(x, mods, w, cos_t, sin_t, tm, pos0):
    b, l, d = x.shape
    nt = l // tm
    p0 = pos0 // tm
    row = lambda bi, li: (bi, li, 0)
    tab = lambda bi, li: (li + p0, 0)
    outs = [(MLA_KV_RANK, F32), (MLA_ROPE, F32), (MLA_HEADS * HEAD_PAD, BF16), (GLA_K_ALL, BF16),
            (GLA_K_ALL, BF16), (GLA_V_ALL, BF16), (GLA_K_ALL, F32), (GLA_V_ALL, BF16)]
    return pl.pallas_call(
        _inproj_kernel,
        grid=(b, nt),
        in_specs=[pl.BlockSpec((1, tm, d), row),
                  pl.BlockSpec((1, 6, d), lambda bi, li: (bi, 0, 0)),
                  _const_spec((1, d)),
                  _const_spec((d, IN_COLS_PAD)),
                  _const_spec((1, MLA_Q_RANK)),
                  _const_spec((MLA_Q_RANK, MLA_HEADS * HEAD_PAD)),
                  _const_spec((3, HEAD_PAD)),
                  pl.BlockSpec((tm, HEAD_PAD), tab),
                  pl.BlockSpec((tm, HEAD_PAD), tab),
                  _const_spec((1, MLA_KV_RANK)),
                  _const_spec((LANES, GLA_K_ALL)),
                  _const_spec((1, GLA_K_ALL))],
        out_specs=[pl.BlockSpec((1, tm, n), row) for n, _ in outs],
        out_shape=[jax.ShapeDtypeStruct((b, l, n), dt) for n, dt in outs],
        compiler_params=_params("arbitrary", "arbitrary"),
        name="inproj",
    )(x, mods, w["g_norm1"], w["w_in"], w["g_qa"], w["w_uq"], w["gq3"], cos_t, sin_t,
      w["g_kva"], w["w_a2"], w["b_a2"])


def _keys_kernel(ckv_ref, kpe_ref, wukv_ref, sel_ref, gk3_ref, cos_ref, sin_ref, k_ref, v_ref):
    kv = jnp.dot(ckv_ref[0].astype(BF16), wukv_ref[...], preferred_element_type=F32)
    kpe = kpe_ref[0]
    kpe_hi = kpe.astype(BF16)
    kpe_lo = (kpe - kpe_hi.astype(F32)).astype(BF16)
    uv = (jnp.dot(kpe_hi, sel_ref[...], preferred_element_type=F32)
          + jnp.dot(kpe_lo, sel_ref[...], preferred_element_type=F32))
    rot = (uv[:, :HEAD_PAD] * (cos_ref[...] * gk3_ref[1:2, :])
           + uv[:, HEAD_PAD:] * (sin_ref[...] * gk3_ref[2:3, :]))
    sp = jnp.sum(kpe * kpe, axis=-1, keepdims=True)
    lane = lax.broadcasted_iota(jnp.int32, (1, HEAD_PAD), 1)
    is_nope = lane < MLA_NOPE
    g_nope = gk3_ref[0:1, :]
    for hd in range(MLA_HEADS):
        blk = kv[:, hd * HEAD_PAD:(hd + 1) * HEAD_PAD]
        kn = jnp.where(is_nope, blk, 0.0)
        ss = jnp.sum(kn * kn, axis=-1, keepdims=True) + sp
        r = lax.rsqrt(ss * (1.0 / MLA_QK) + EPS)
        k_ref[0, :, hd * HEAD_PAD:(hd + 1) * HEAD_PAD] = ((kn * g_nope + rot) * r).astype(BF16)
        v_ref[0, :, hd * HEAD_PAD:(hd + 1) * HEAD_PAD] = jnp.where(is_nope, 1.0, blk).astype(BF16)


def _keys(ckv, kpe, w, cos_t, sin_t, tm):
    b, l, _ = ckv.shape
    row = lambda bi, li: (bi, li, 0)
    tab = lambda bi, li: (li, 0)
    hw = MLA_HEADS * HEAD_PAD
    return pl.pallas_call(
        _keys_kernel,
        grid=(b, l // tm),
        in_specs=[pl.BlockSpec((1, tm, MLA_KV_RANK), row),
                  pl.BlockSpec((1, tm, MLA_ROPE), row),
                  _const_spec((MLA_KV_RANK, hw)),
                  _const_spec((MLA_ROPE, 2 * HEAD_PAD)),
                  _const_spec((3, HEAD_PAD)),
                  pl.BlockSpec((tm, HEAD_PAD), tab),
                  pl.BlockSpec((tm, HEAD_PAD), tab)],
        out_specs=[pl.BlockSpec((1, tm, hw), row), pl.BlockSpec((1, tm, hw), row)],
        out_shape=[jax.ShapeDtypeStruct((b, l, hw), BF16)] * 2,
        compiler_params=_params("arbitrary", "arbitrary"),
        name="keys",
    )(ckv, kpe, w["w_ukv"], w["sel"], w["gk3"], cos_t, sin_t)


_NT = (((1,), (1,)), ((), ()))
_TN = (((0,), (0,)), ((), ()))
MASKED = -1e30


def _attn_prompt_kernel(q_ref, k_ref, v_ref, o_ref, acc_ref, m_ref, *, tq):
    i = pl.program_id(2)
    q = q_ref[0]
    acc_ref[...] = jnp.zeros_like(acc_ref)
    m_ref[...] = jnp.full_like(m_ref, -jnp.inf)

    def step(j, masked):
        start = pl.multiple_of(j * tq, tq)
        kb = k_ref[0, pl.ds(start, tq), :]
        vb = v_ref[0, pl.ds(start, tq), :]
        s = lax.dot_general(q, kb, _NT, preferred_element_type=F32)
        if masked:
            rq = lax.broadcasted_iota(jnp.int32, (tq, tq), 0) >> CHUNK_SHIFT
            ck = lax.broadcasted_iota(jnp.int32, (tq, tq), 1) >> CHUNK_SHIFT
            s = jnp.where(ck <= rq, s, MASKED)
        m_old = m_ref[...]
        m_new = jnp.maximum(m_old, jnp.max(s, axis=-1, keepdims=True))
        p = jnp.exp(s - m_new)
        acc_ref[...] = (jnp.exp(m_old - m_new) * acc_ref[...]
                        + jnp.dot(p.astype(BF16), vb, preferred_element_type=F32))
        m_ref[...] = m_new

    def body(j, carry):
        step(j, False)
        return carry

    lax.fori_loop(0, i, body, 0)
    step(i, True)
    acc = acc_ref[...]
    o_ref[0] = (acc / acc[:, 0:1]).astype(BF16)


def _attn_prompt(q, k, v, tq):
    b, l, hw = q.shape
    qmap = lambda bi, hi, qi: (bi, qi, hi)
    kmap = lambda bi, hi, qi: (bi, 0, hi)
    return pl.pallas_call(
        functools.partial(_attn_prompt_kernel, tq=tq),
        grid=(b, MLA_HEADS, l // tq),
        in_specs=[pl.BlockSpec((1, tq, HEAD_PAD), qmap),
                  pl.BlockSpec((1, l, HEAD_PAD), kmap),
                  pl.BlockSpec((1, l, HEAD_PAD), kmap)],
        out_specs=pl.BlockSpec((1, tq, HEAD_PAD), qmap),
        out_shape=jax.ShapeDtypeStruct((b, l, hw), BF16),
        scratch_shapes=[pltpu.VMEM((tq, HEAD_PAD), F32), pltpu.VMEM((tq, 1), F32)],
        compiler_params=_params("arbitrary", "arbitrary", "arbitrary"),
        name="attn_prompt",
    )(q, k, v)


def _attn_sample_kernel(q_ref, k_ref, v_ref, o_ref, *, n_valid):
    s = lax.dot_general(q_ref[0], k_ref[0], _NT, preferred_element_type=F32)
    col = lax.broadcasted_iota(jnp.int32, s.shape, 1)
    s = jnp.where(col < n_valid, s, MASKED)
    p = jnp.exp(s - jnp.max(s, axis=-1, keepdims=True))
    acc = jnp.dot(p.astype(BF16), v_ref[0], preferred_element_type=F32)
    o_ref[0] = (acc / acc[:, 0:1]).astype(BF16)


def _attn_sample(q, k, v, n_valid):
    b, t, hw = q.shape
    lk = k.shape[1]
    qmap = lambda bi, hi: (bi, 0, hi)
    return pl.pallas_call(
        functools.partial(_attn_sample_kernel, n_valid=n_valid),
        grid=(b, MLA_HEADS),
        in_specs=[pl.BlockSpec((1, t, HEAD_PAD), qmap),
                  pl.BlockSpec((1, lk, HEAD_PAD), qmap),
                  pl.BlockSpec((1, lk, HEAD_PAD), qmap)],
        out_specs=pl.BlockSpec((1, t, HEAD_PAD), qmap),
        out_shape=jax.ShapeDtypeStruct((b, t, hw), BF16),
        compiler_params=_params("arbitrary", "arbitrary"),
        name="attn_sample",
    )(q, k, v)


def _gla_kernel(gq_ref, gk_ref, gv_ref, lg_ref, og_ref, tri_ref, ggla_ref, st0_ref,
                o_ref, stf_ref, st_scr, b_scr, *, tg):
    li = pl.program_id(1)

    @pl.when(li == 0)
    def _():
        st_scr[...] = st0_ref[0]

    lg = lg_ref[0]
    lg_hi = lg.astype(BF16)
    lg_lo = (lg - lg_hi.astype(F32)).astype(BF16)
    tri = tri_ref[...]
    b_scr[...] = (jnp.dot(tri, lg_hi, preferred_element_type=F32)
                  + jnp.dot(tri, lg_lo, preferred_element_type=F32))

    lane = lax.broadcasted_iota(jnp.int32, (1, GLA_K_ALL), 1)
    head_of_lane = lane >> CHUNK_SHIFT
    ri = lax.broadcasted_iota(jnp.int32, (GLA_HEADS * CHUNK, CHUNK), 0) & (CHUNK - 1)
    cj = lax.broadcasted_iota(jnp.int32, (GLA_HEADS * CHUNK, CHUNK), 1)
    causal = cj <= ri
    g_out = ggla_ref[...]

    for c in range(tg // CHUNK):
        r0 = c * CHUNK
        b = b_scr[r0:r0 + CHUNK, :]
        b_mid = b[CHUNK // 2 - 1:CHUNK // 2, :]
        b_last = b[CHUNK - 1:CHUNK, :]
        q = gq_ref[0, r0:r0 + CHUNK, :].astype(F32)
        k = gk_ref[0, r0:r0 + CHUNK, :].astype(F32)
        qe = q * jnp.exp(jnp.minimum(b - b_mid, EXP_CLAMP))
        ke = (k * jnp.exp(jnp.minimum(b_mid - b, EXP_CLAMP))).astype(BF16)
        qb = q * jnp.exp(b)
        kd = k * jnp.exp(b_last - b)
        zero = jnp.zeros_like(q)
        qe_st = jnp.concatenate([jnp.where(head_of_lane == hd, qe, zero) for hd in range(GLA_HEADS)],
                                axis=0).astype(BF16)
        qb_st = jnp.concatenate([jnp.where(head_of_lane == hd, qb, zero) for hd in range(GLA_HEADS)],
                                axis=0).astype(BF16)
        a_st = lax.dot_general(qe_st, ke, _NT, preferred_element_type=F32)
        a_st = jnp.where(causal, a_st, 0.0).astype(BF16)
        st = st_scr[...]
        o_inter = lax.dot_general(qb_st, st.astype(BF16), _NT, preferred_element_type=F32)
        upd = jnp.zeros_like(st)
        for hd in range(GLA_HEADS):
            vh = gv_ref[0, r0:r0 + CHUNK, hd * GLA_DV:(hd + 1) * GLA_DV]
            o = (o_inter[hd * CHUNK:(hd + 1) * CHUNK, :]
                 + jnp.dot(a_st[hd * CHUNK:(hd + 1) * CHUNK, :], vh, preferred_element_type=F32))
            og = og_ref[0, r0:r0 + CHUNK, hd * GLA_DV:(hd + 1) * GLA_DV].astype(F32)
            o = _rms_rows(o) * g_out * (og * jax.nn.sigmoid(og))
            o_ref[0, r0:r0 + CHUNK, hd * GLA_DV:(hd + 1) * GLA_DV] = o.astype(BF16)
            kd_h = jnp.where(head_of_lane == hd, kd, zero).astype(BF16)
            upd = upd + lax.dot_general(vh, kd_h, _TN, preferred_element_type=F32)
        st_scr[...] = st * jnp.exp(b_last) + upd

    stf_ref[0] = st_scr[...]


def _gla(gq, gk, gv, lg, og, st0, w, tg):
    b, l, _ = gq.shape
    row = lambda bi, li: (bi, li, 0)
    fix = lambda bi, li: (bi, 0, 0)
    blk = jnp.arange(tg) // CHUNK
    tri = ((blk[:, None] == blk[None, :]) & (jnp.arange(tg)[None, :] <= jnp.arange(tg)[:, None])).astype(BF16)
    return pl.pallas_call(
        functools.partial(_gla_kernel, tg=tg),
        grid=(b, l // tg),
        in_specs=[pl.BlockSpec((1, tg, GLA_K_ALL), row),
                  pl.BlockSpec((1, tg, GLA_K_ALL), row),
                  pl.BlockSpec((1, tg, GLA_V_ALL), row),
                  pl.BlockSpec((1, tg, GLA_K_ALL), row),
                  pl.BlockSpec((1, tg, GLA_V_ALL), row),
                  _const_spec((tg, tg)),
                  _const_spec((1, GLA_DV)),
                  pl.BlockSpec((1, GLA_DV, GLA_K_ALL), fix)],
        out_specs=[pl.BlockSpec((1, tg, GLA_V_ALL), row),
                   pl.BlockSpec((1, GLA_DV, GLA_K_ALL), fix)],
        out_shape=[jax.ShapeDtypeStruct((b, l, GLA_V_ALL), BF16),
                   jax.ShapeDtypeStruct((b, GLA_DV, GLA_K_ALL), F32)],
        scratch_shapes=[pltpu.VMEM((GLA_DV, GLA_K_ALL), F32), pltpu.VMEM((tg, GLA_K_ALL), F32)],
        compiler_params=_params("arbitrary", "arbitrary"),
        name="gla",
    )(gq, gk, gv, lg, og, tri, w["g_gla"], st0)


def _ffn_kernel(x_ref, om_ref, ogl_ref, mods_ref, g2_ref, wom_ref, wog_ref, wup_ref, wcv_ref, bcv_ref,
                wdn_ref, hist_ref, y_ref, nh_ref, carry_scr, a_scr, acc_scr, *, tm):
    li = pl.program_id(1)
    hist_rows = CONV_W - 1
    pad = 8

    @pl.when(li == 0)
    def _():
        carry_scr[...] = jnp.zeros_like(carry_scr)
        carry_scr[pad - hist_rows:pad, :] = hist_ref[0]

    mixed = (jnp.dot(om_ref[0], wom_ref[...], preferred_element_type=F32)
             + jnp.dot(ogl_ref[0], wog_ref[...], preferred_element_type=F32))
    x1 = x_ref[0] + mods_ref[0, 2:3, :] * mixed
    h = (_rms_rows(x1) * g2_ref[...] * (1.0 + mods_ref[0, 4:5, :]) + mods_ref[0, 3:4, :]).astype(BF16)

    for f in range(FFN_DIM // FFN_TILE):
        c0 = f * FFN_TILE
        a = jnp.dot(h, wup_ref[:, c0:c0 + FFN_TILE], preferred_element_type=F32)
        g = jnp.dot(h, wup_ref[:, FFN_DIM + c0:FFN_DIM + c0 + FFN_TILE], preferred_element_type=F32)
        a_scr[0:pad, :] = carry_scr[:, c0:c0 + FFN_TILE]
        a_scr[pad:pad + tm, :] = a
        carry_scr[:, c0:c0 + FFN_TILE] = a[tm - pad:tm, :]
        nh_ref[0, :, c0:c0 + FFN_TILE] = a[tm - hist_rows:tm, :]
        conv = (bcv_ref[:, c0:c0 + FFN_TILE]
                + wcv_ref[2:3, c0:c0 + FFN_TILE] * a
                + wcv_ref[1:2, c0:c0 + FFN_TILE] * a_scr[pad - 1:pad - 1 + tm, :]
                + wcv_ref[0:1, c0:c0 + FFN_TILE] * a_scr[pad - 2:pad - 2 + tm, :])
        act = (jax.nn.gelu(conv) * g).astype(BF16)
        y = jnp.dot(act, wdn_ref[c0:c0 + FFN_TILE, :], preferred_element_type=F32)
        if f == 0:
            acc_scr[...] = y
        else:
            acc_scr[...] += y

    y_ref[0] = x1 + mods_ref[0, 5:6, :] * acc_scr[...]


def _ffn(x, o_mla, o_gla, mods, hist, w, tm):
    b, l, d = x.shape
    row = lambda bi, li: (bi, li, 0)
    fix = lambda bi, li: (bi, 0, 0)
    hw = MLA_HEADS * HEAD_PAD
    return pl.pallas_call(
        functools.partial(_ffn_kernel, tm=tm),
        grid=(b, l // tm),
        in_specs=[pl.BlockSpec((1, tm, d), row),
                  pl.BlockSpec((1, tm, hw), row),
                  pl.BlockSpec((1, tm, GLA_V_ALL), row),
                  pl.BlockSpec((1, 6, d), fix),
                  _const_spec((1, d)),
                  _const_spec((hw, d)),
                  _const_spec((GLA_V_ALL, d)),
                  _const_spec((d, 2 * FFN_DIM)),
                  _const_spec((CONV_W, FFN_DIM)),
                  _const_spec((1, FFN_DIM)),
                  _const_spec((FFN_DIM, d)),
                  pl.BlockSpec((1, CONV_W - 1, FFN_DIM), fix)],
        out_specs=[pl.BlockSpec((1, tm, d), row),
                   pl.BlockSpec((1, CONV_W - 1, FFN_DIM), fix)],
        out_shape=[jax.ShapeDtypeStruct((b, l, d), F32),
                   jax.ShapeDtypeStruct((b, CONV_W - 1, FFN_DIM), F32)],
        scratch_shapes=[pltpu.VMEM((8, FFN_DIM), F32),
                        pltpu.VMEM((tm + 8, FFN_TILE), F32),
                        pltpu.VMEM((tm, d), F32)],
        compiler_params=_params("arbitrary", "arbitrary"),
        name="ffn",
    )(x, o_mla, o_gla, mods, w["g_norm2"], w["w_out_mla"], w["w_out_gla"], w["w_up"], w["w_conv"],
      w["b_conv"], w["w_down"], hist)


def _prep_weights(w_in, g_norm1, g_qa, w_uq, g_qn, g_kva, w_ukv, g_kn, w_a2, b_a2, g_gla, w_out,
                  g_norm2, w_up, w_conv, b_conv, w_down):
    o, cols = 0, []
    for n in (MLA_Q_RANK, MLA_KV_RANK, MLA_ROPE, GLA_K_ALL, GLA_K_ALL, GLA_V_ALL, GLA_GATE_RANK, GLA_V_ALL):
        cols.append(w_in[:, o:o + n])
        o += n
    q_lat, kv_lat, kpe, gq, gk, gv, g_lr, og = cols
    misc_pad = jnp.zeros((D_MODEL, LANES - MLA_ROPE - GLA_GATE_RANK), w_in.dtype)
    w_in_r = jnp.concatenate([q_lat, kv_lat, gq, gk, gv, og, kpe, g_lr, misc_pad], axis=1).astype(BF16)

    wq = w_uq.reshape(MLA_Q_RANK, MLA_HEADS, MLA_QK)
    n_, r1, r2 = wq[..., :MLA_NOPE], wq[..., MLA_NOPE:MLA_NOPE + HALF_ROPE], wq[..., MLA_NOPE + HALF_ROPE:]
    w_uq_p = jnp.concatenate([n_, r1, r2, r2, r1], axis=-1).reshape(MLA_Q_RANK, MLA_HEADS * HEAD_PAD).astype(BF16)

    sc = MLA_QK ** -0.5
    gn, g1, g2 = g_qn[:MLA_NOPE], g_qn[MLA_NOPE:MLA_NOPE + HALF_ROPE], g_qn[MLA_NOPE + HALF_ROPE:]
    z16, z64 = jnp.zeros((HALF_ROPE,), F32), jnp.zeros((MLA_NOPE,), F32)
    gq3 = jnp.stack([jnp.concatenate([gn, z16, z16, z16, z16]),
                     jnp.concatenate([z64, g1, g2, z16, z16]),
                     jnp.concatenate([z64, z16, z16, g2, g1])]) * sc
    kn, k1, k2 = g_kn[:MLA_NOPE], g_kn[MLA_NOPE:MLA_NOPE + HALF_ROPE], g_kn[MLA_NOPE + HALF_ROPE:]
    gk3 = jnp.stack([jnp.concatenate([kn, z16, z16, z16, z16]),
                     jnp.concatenate([z64, k1, k2, -k1, k2]),
                     jnp.concatenate([z64, -k2, k1, k2, k1])])
    eye = jnp.eye(HALF_ROPE, dtype=F32)
    zz = jnp.zeros((HALF_ROPE, HALF_ROPE), F32)
    pick1 = jnp.concatenate([eye, zz], axis=0)
    pick2 = jnp.concatenate([zz, eye], axis=0)
    z_n = jnp.zeros((MLA_ROPE, MLA_NOPE), F32)
    sel = jnp.concatenate([z_n, pick1, pick2, pick1, pick2, z_n, pick2, pick1, pick2, pick1], axis=1).astype(BF16)

    w_a2_p = jnp.zeros((LANES, GLA_K_ALL), F32).at[MLA_ROPE:MLA_ROPE + GLA_GATE_RANK].set(w_a2).astype(BF16)

    wo = w_out[:MLA_HEADS * MLA_V].reshape(MLA_HEADS, MLA_V, D_MODEL)
    w_out_mla = jnp.concatenate([jnp.zeros_like(wo), wo], axis=1).reshape(MLA_HEADS * HEAD_PAD, D_MODEL).astype(BF16)
    return {
        "g_norm1": g_norm1.reshape(1, -1), "w_in": w_in_r, "g_qa": g_qa.reshape(1, -1), "w_uq": w_uq_p,
        "gq3": gq3, "g_kva": g_kva.reshape(1, -1), "w_ukv": w_ukv.astype(BF16), "sel": sel, "gk3": gk3,
        "w_a2": w_a2_p, "b_a2": b_a2.reshape(1, -1), "g_gla": g_gla.reshape(1, -1),
        "w_out_mla": w_out_mla, "w_out_gla": w_out[MLA_HEADS * MLA_V:].astype(BF16),
        "g_norm2": g_norm2.reshape(1, -1), "w_up": w_up.astype(BF16), "w_conv": w_conv,
        "b_conv": b_conv.reshape(1, -1), "w_down": w_down.astype(BF16),
    }


def _rope_tables(n):
    inv = 1.0 / (ROPE_THETA ** (jnp.arange(HALF_ROPE, dtype=F32) / HALF_ROPE))
    ang = jnp.arange(n, dtype=F32)[:, None] * jnp.tile(inv, LANES // HALF_ROPE)[None, :]
    return jnp.cos(ang), jnp.sin(ang)


def _state_to_t(s):
    b = s.shape[0]
    return s.reshape(b, GLA_K_ALL, GLA_DV).transpose(0, 2, 1)


def _state_from_t(st):
    b = st.shape[0]
    return st.transpose(0, 2, 1).reshape(b, GLA_HEADS, GLA_DK, GLA_DV)


def _pick_tile(n, pref):
    t = min(n, pref)
    while n % t:
        t //= 2
    return t


def _layer(x, mods, w, cos_t, sin_t, cache_ckv, cache_kpe, state, hist):
    b, l, _ = x.shape
    tm = _pick_tile(l, 512)
    pos0 = 0 if cache_ckv is None else cache_ckv.shape[1]
    ckv, kpe, q, gq, gk, gv, lg, og = _inproj(x, mods, w, cos_t, sin_t, tm, pos0)
    if cache_ckv is None:
        k, v = _keys(ckv, kpe, w, cos_t, sin_t, tm)
        o_mla = _attn_prompt(q, k, v, tm)
    else:
        n_valid = pos0 + l
        lk = -(-n_valid // LANES) * LANES
        ckv_all = jnp.concatenate([cache_ckv, ckv, jnp.zeros((b, lk - n_valid, MLA_KV_RANK), F32)], axis=1)
        kpe_all = jnp.concatenate([cache_kpe, kpe, jnp.zeros((b, lk - n_valid, MLA_ROPE), F32)], axis=1)
        k, v = _keys(ckv_all, kpe_all, w, cos_t, sin_t, lk)
        o_mla = _attn_sample(q, k, v, n_valid)
    o_gla, st = _gla(gq, gk, gv, lg, og, _state_to_t(state), w, tm)
    y, new_hist = _ffn(x, o_mla, o_gla, mods, hist, w, tm)
    return y, ckv, kpe, _state_from_t(st), new_hist


def kernel(x_prompt, x_sample, c_prompt, c_sample, cache_ckv, cache_kpe, state_gla, state_ffn_conv, w_ada, b_ada, g_norm1, w_in, g_qa, w_uq, g_qn, g_kva, w_ukv, g_kn, w_a2, b_a2, g_gla, w_out, g_norm2, w_up, w_conv, b_conv, w_down):
    depth = w_ada.shape[0]
    bp, lp, _ = x_prompt.shape
    bs, ls, _ = x_sample.shape
    n_pos = max(lp, cache_ckv.shape[2] + ls)
    n_pos = -(-n_pos // LANES) * LANES
    cos_t, sin_t = _rope_tables(n_pos)
    yp, ys = x_prompt, x_sample
    outs = [[] for _ in range(8)]
    for i in range(depth):
        w = _prep_weights(w_in[i], g_norm1[i], g_qa[i], w_uq[i], g_qn[i], g_kva[i], w_ukv[i], g_kn[i],
                          w_a2[i], b_a2[i], g_gla[i], w_out[i], g_norm2[i], w_up[i], w_conv[i], b_conv[i],
                          w_down[i])
        mods = _adaln(jnp.concatenate([c_prompt, c_sample], axis=0), w_ada[i], b_ada[i])
        mods = mods.reshape(bp + bs, 6, D_MODEL)
        zero_state = jnp.zeros((bp, GLA_HEADS, GLA_DK, GLA_DV), F32)
        zero_hist = jnp.zeros((bp, CONV_W - 1, FFN_DIM), F32)
        yp, a, b_, s, h = _layer(yp, mods[:bp], w, cos_t, sin_t, None, None, zero_state, zero_hist)
        for lst, val in zip(outs[:4], (a, b_, s, h)):
            lst.append(val)
        ys, a, b_, s, h = _layer(ys, mods[bp:], w, cos_t, sin_t, cache_ckv[i], cache_kpe[i],
                                 state_gla[i], state_ffn_conv[i])
        for lst, val in zip(outs[4:], (a, b_, s, h)):
            lst.append(val)
    return (yp, ys) + tuple(jnp.stack(o) for o in outs)
```

```python
import functools
import math

import jax
import jax.numpy as jnp
from jax import lax
from jax.experimental import pallas as pl
from jax.experimental.pallas import tpu as pltpu

F32 = jnp.float32
BF16 = jnp.bfloat16

D_MODEL = 1024
CHUNK = 64
CHUNK_SHIFT = 6
MLA_HEADS = 8
MLA_NOPE = 64
MLA_ROPE = 32
MLA_QK = MLA_NOPE + MLA_ROPE
MLA_V = 64
MLA_Q_RANK = 384
MLA_KV_RANK = 256
ROPE_THETA = 10000.0
GLA_HEADS = 4
GLA_DK = 64
GLA_DV = 128
GLA_GATE_RANK = 16
GLA_TAU = 16.0
FFN_DIM = 2816
CONV_W = 3
EPS = 1e-6

LANES = 128
HEAD_PAD = LANES
HALF_ROPE = MLA_ROPE // 2
GLA_K_ALL = GLA_HEADS * GLA_DK
GLA_V_ALL = GLA_HEADS * GLA_DV
C_QLAT = 0
C_KV = C_QLAT + MLA_Q_RANK
C_GQ = C_KV + MLA_KV_RANK
C_GK = C_GQ + GLA_K_ALL
C_GV = C_GK + GLA_K_ALL
C_OG = C_GV + GLA_V_ALL
C_MISC = C_OG + GLA_V_ALL
IN_COLS_PAD = C_MISC + LANES
FFN_TILE = 256
EXP_CLAMP = 80.0
VMEM_LIMIT = 56 * 1024 * 1024


def _const_spec(shape):
    nd = len(shape)
    return pl.BlockSpec(shape, lambda *_: (0,) * nd, pipeline_mode=pl.Buffered(1))


def _params(*sem):
    return pltpu.CompilerParams(dimension_semantics=sem, vmem_limit_bytes=VMEM_LIMIT)


def _rms_rows(x):
    return x * lax.rsqrt(jnp.mean(x * x, axis=-1, keepdims=True) + EPS)


def _ada_kernel(c_ref, w_ref, b_ref, o_ref):
    c = c_ref[...]
    s = c * jax.nn.sigmoid(c)
    o_ref[...] = jnp.dot(s, w_ref[...], precision=lax.Precision.HIGHEST,
                         preferred_element_type=F32) + b_ref[...]


def _adaln(c, w_ada, b_ada):
    n, d = c.shape
    cols = w_ada.shape[1]
    tn = 1024
    return pl.pallas_call(
        _ada_kernel,
        grid=(cols // tn,),
        in_specs=[pl.BlockSpec((n, d), lambda j: (0, 0)),
                  pl.BlockSpec((d, tn), lambda j: (0, j)),
                  pl.BlockSpec((1, tn), lambda j: (0, j))],
        out_specs=pl.BlockSpec((n, tn), lambda j: (0, j)),
        out_shape=jax.ShapeDtypeStruct((n, cols), F32),
        compiler_params=_params("arbitrary"),
        name="adaln",
    )(c, w_ada, b_ada.reshape(1, cols))


def _inproj_kernel(x_ref, mods_ref, g1_ref, win_ref, gqa_ref, wuq_ref, gq3_ref, cos_ref, sin_ref,
                   gkva_ref, wa2_ref, ba2_ref,
                   ckv_ref, kpe_ref, q_ref, gq_ref, gk_ref, gv_ref, lg_ref, og_ref):
    x = x_ref[0]
    shift = mods_ref[0, 0:1, :]
    scale = mods_ref[0, 1:2, :]
    h = _rms_rows(x) * g1_ref[...] * (1.0 + scale) + shift
    p = jnp.dot(h.astype(BF16), win_ref[...], preferred_element_type=F32)

    ckv_ref[0] = _rms_rows(p[:, C_KV:C_GQ]) * gkva_ref[...]
    misc = p[:, C_MISC:IN_COLS_PAD]
    kpe_ref[0] = misc[:, 0:MLA_ROPE]

    qa = _rms_rows(p[:, C_QLAT:C_KV]) * gqa_ref[...]
    qu = jnp.dot(qa.astype(BF16), wuq_ref[...], preferred_element_type=F32)
    tq = gq3_ref[0:1, :] + cos_ref[...] * gq3_ref[1:2, :] + sin_ref[...] * gq3_ref[2:3, :]
    lane = lax.broadcasted_iota(jnp.int32, (1, HEAD_PAD), 1)
    real = (lane < MLA_QK).astype(F32)
    for hd in range(MLA_HEADS):
        xh = qu[:, hd * HEAD_PAD:(hd + 1) * HEAD_PAD]
        ss = jnp.sum(xh * xh * real, axis=-1, keepdims=True)
        r = lax.rsqrt(ss * (1.0 / MLA_QK) + EPS)
        q_ref[0, :, hd * HEAD_PAD:(hd + 1) * HEAD_PAD] = (xh * r * tq).astype(BF16)

    gq_ref[0] = (p[:, C_GQ:C_GK] * (GLA_DK ** -0.5)).astype(BF16)
    gk_ref[0] = p[:, C_GK:C_GV].astype(BF16)
    gv_ref[0] = p[:, C_GV:C_OG].astype(BF16)
    og_ref[0] = p[:, C_OG:C_MISC].astype(BF16)
    z = jnp.dot(misc.astype(BF16), wa2_ref[...], preferred_element_type=F32) + ba2_ref[...]
    log_sig = jnp.minimum(z, 0.0) - jnp.log1p(jnp.exp(-jnp.abs(z)))
    lg_ref[0] = log_sig * (1.0 / GLA_TAU)


def _inproj(x, mods, w, cos_t, sin_t, tm, pos0):
    b, l, d = x.shape
    nt = l // tm
    p0 = pos0 // tm
    row = lambda bi, li: (bi, li, 0)
    tab = lambda bi, li: (li + p0, 0)
    outs = [(MLA_KV_RANK, F32), (MLA_ROPE, F32), (MLA_HEADS * HEAD_PAD, BF16), (GLA_K_ALL, BF16),
            (GLA_K_ALL, BF16), (GLA_V_ALL, BF16), (GLA_K_ALL, F32), (GLA_V_ALL, BF16)]
    return pl.pallas_call(
        _inproj_kernel,
        grid=(b, nt),
        in_specs=[pl.BlockSpec((1, tm, d), row),
                  pl.BlockSpec((1, 6, d), lambda bi, li: (bi, 0, 0)),
                  _const_spec((1, d)),
                  _const_spec((d, IN_COLS_PAD)),
                  _const_spec((1, MLA_Q_RANK)),
                  _const_spec((MLA_Q_RANK, MLA_HEADS * HEAD_PAD)),
                  _const_spec((3, HEAD_PAD)),
                  pl.BlockSpec((tm, HEAD_PAD), tab),
                  pl.BlockSpec((tm, HEAD_PAD), tab),
                  _const_spec((1, MLA_KV_RANK)),
                  _const_spec((LANES, GLA_K_ALL)),
                  _const_spec((1, GLA_K_ALL))],
        out_specs=[pl.BlockSpec((1, tm, n), row) for n, _ in outs],
        out_shape=[jax.ShapeDtypeStruct((b, l, n), dt) for n, dt in outs],
        compiler_params=_params("arbitrary", "arbitrary"),
        name="inproj",
    )(x, mods, w["g_norm1"], w["w_in"], w["g_qa"], w["w_uq"], w["gq3"], cos_t, sin_t,
      w["g_kva"], w["w_a2"], w["b_a2"])


def _keys_kernel(ckv_ref, kpe_ref, wukv_ref, sel_ref, gk3_ref, cos_ref, sin_ref, k_ref, v_ref, *,
                 transpose_v):
    kv = jnp.dot(ckv_ref[0].astype(BF16), wukv_ref[...], preferred_element_type=F32)
    kpe = kpe_ref[0]
    kpe_hi = kpe.astype(BF16)
    kpe_lo = (kpe - kpe_hi.astype(F32)).astype(BF16)
    uv = (jnp.dot(kpe_hi, sel_ref[...], preferred_element_type=F32)
          + jnp.dot(kpe_lo, sel_ref[...], preferred_element_type=F32))
    rot = (uv[:, :HEAD_PAD] * (cos_ref[...] * gk3_ref[1:2, :])
           + uv[:, HEAD_PAD:] * (sin_ref[...] * gk3_ref[2:3, :]))
    sp = jnp.sum(kpe * kpe, axis=-1, keepdims=True)
    lane = lax.broadcasted_iota(jnp.int32, (1, HEAD_PAD), 1)
    is_nope = lane < MLA_NOPE
    g_nope = gk3_ref[0:1, :]
    for hd in range(MLA_HEADS):
        blk = kv[:, hd * HEAD_PAD:(hd + 1) * HEAD_PAD]
        kn = jnp.where(is_nope, blk, 0.0)
        ss = jnp.sum(kn * kn, axis=-1, keepdims=True) + sp
        r = lax.rsqrt(ss * (1.0 / MLA_QK) + EPS)
        k_ref[0, :, hd * HEAD_PAD:(hd + 1) * HEAD_PAD] = ((kn * g_nope + rot) * r).astype(BF16)
        vp = jnp.where(is_nope, 1.0, blk)
        if transpose_v:
            v_ref[0, hd * HEAD_PAD:(hd + 1) * HEAD_PAD, :] = vp.T.astype(BF16)
        else:
            v_ref[0, :, hd * HEAD_PAD:(hd + 1) * HEAD_PAD] = vp.astype(BF16)


def _keys(ckv, kpe, w, cos_t, sin_t, tm, transpose_v):
    b, l, _ = ckv.shape
    row = lambda bi, li: (bi, li, 0)
    tab = lambda bi, li: (li, 0)
    hw = MLA_HEADS * HEAD_PAD
    if transpose_v:
        v_spec = pl.BlockSpec((1, hw, tm), lambda bi, li: (bi, 0, li))
        v_shape = jax.ShapeDtypeStruct((b, hw, l), BF16)
    else:
        v_spec = pl.BlockSpec((1, tm, hw), row)
        v_shape = jax.ShapeDtypeStruct((b, l, hw), BF16)
    return pl.pallas_call(
        functools.partial(_keys_kernel, transpose_v=transpose_v),
        grid=(b, l // tm),
        in_specs=[pl.BlockSpec((1, tm, MLA_KV_RANK), row),
                  pl.BlockSpec((1, tm, MLA_ROPE), row),
                  _const_spec((MLA_KV_RANK, hw)),
                  _const_spec((MLA_ROPE, 2 * HEAD_PAD)),
                  _const_spec((3, HEAD_PAD)),
                  pl.BlockSpec((tm, HEAD_PAD), tab),
                  pl.BlockSpec((tm, HEAD_PAD), tab)],
        out_specs=[pl.BlockSpec((1, tm, hw), row), v_spec],
        out_shape=[jax.ShapeDtypeStruct((b, l, hw), BF16), v_shape],
        compiler_params=_params("arbitrary", "arbitrary"),
        name="keys",
    )(ckv, kpe, w["w_ukv"], w["sel"], w["gk3"], cos_t, sin_t)


_NT = (((1,), (1,)), ((), ()))
_TN = (((0,), (0,)), ((), ()))
MASKED = -1e30


def _attn_prompt_kernel(i_tab, j_tab, q_ref, k_ref, vt_ref, o_ref, s_buf, p_buf, a_buf, m_scr, acc_scr,
                        *, tq, n_steps, n_iter):
    s_buf[...] = jnp.zeros_like(s_buf)
    p_buf[...] = jnp.zeros_like(p_buf)
    a_buf[...] = jnp.zeros_like(a_buf)
    m_scr[...] = jnp.zeros_like(m_scr)
    acc_scr[...] = jnp.zeros_like(acc_scr)

    def sub_step(t, slot):
        other = 1 - slot
        t2 = jnp.maximum(t - 2, 0)
        i2, j2 = i_tab[t2], j_tab[t2]
        vt = vt_ref[0, :, pl.ds(pl.multiple_of(j2 * tq, tq), tq)]
        acc_scr[...] = a_buf[slot] * acc_scr[...] + jnp.dot(vt, p_buf[slot], preferred_element_type=F32)
        t1 = jnp.maximum(t - 1, 0)
        s = s_buf[other]
        m_old = jnp.where(j_tab[t1] == 0, -jnp.inf, m_scr[...])
        m_new = jnp.maximum(m_old, jnp.max(s, axis=0, keepdims=True))
        a_buf[other] = jnp.exp2(m_old - m_new)
        p_buf[other] = jnp.exp2(s - m_new).astype(BF16)
        m_scr[...] = m_new
        t0 = jnp.minimum(t, n_steps - 1)
        i0, j0 = i_tab[t0], j_tab[t0]
        kb = k_ref[0, pl.ds(pl.multiple_of(j0 * tq, tq), tq), :]
        qt = q_ref[0, pl.ds(pl.multiple_of(i0 * tq, tq), tq), :]
        s_buf[slot] = lax.dot_general(kb, qt, _NT, preferred_element_type=F32)

        @pl.when(i0 == j0)
        def _():
            kc = lax.broadcasted_iota(jnp.int32, (tq, tq), 0) >> CHUNK_SHIFT
            qc = lax.broadcasted_iota(jnp.int32, (tq, tq), 1) >> CHUNK_SHIFT
            s_buf[slot] = jnp.where(kc <= qc, s_buf[slot], MASKED)

        @pl.when(jnp.logical_and(jnp.logical_and(t >= 2, t - 2 < n_steps), i2 == j2))
        def _():
            acc = acc_scr[...]
            o_ref[0, pl.ds(pl.multiple_of(i2 * tq, tq), tq), :] = (acc / acc[0:1, :]).T.astype(BF16)

    def body(it, carry):
        sub_step(2 * it, 0)
        sub_step(2 * it + 1, 1)
        return carry

    lax.fori_loop(0, n_iter, body, 0)


def _attn_prompt(q, k, vt, tq):
    b, l, hw = q.shape
    nq = l // tq
    pairs = [(i, j) for i in range(nq) for j in range(i + 1)]
    n_steps = len(pairs)
    n_iter = (n_steps + 3) // 2
    pairs = pairs + [pairs[-1]] * (2 * n_iter - n_steps)
    i_tab = jnp.asarray([p_[0] for p_ in pairs], jnp.int32)
    j_tab = jnp.asarray([p_[1] for p_ in pairs], jnp.int32)
    qmap = lambda bi, hi, *_: (bi, 0, hi)
    grid_spec = pltpu.PrefetchScalarGridSpec(
        num_scalar_prefetch=2,
        grid=(b, MLA_HEADS),
        in_specs=[pl.BlockSpec((1, l, HEAD_PAD), qmap),
                  pl.BlockSpec((1, l, HEAD_PAD), qmap),
                  pl.BlockSpec((1, HEAD_PAD, l), lambda bi, hi, *_: (bi, hi, 0))],
        out_specs=pl.BlockSpec((1, l, HEAD_PAD), qmap),
        scratch_shapes=[pltpu.VMEM((2, tq, tq), F32), pltpu.VMEM((2, tq, tq), BF16),
                        pltpu.VMEM((2, 1, tq), F32), pltpu.VMEM((1, tq), F32),
                        pltpu.VMEM((HEAD_PAD, tq), F32)])
    return pl.pallas_call(
        functools.partial(_attn_prompt_kernel, tq=tq, n_steps=n_steps, n_iter=n_iter),
        grid_spec=grid_spec,
        out_shape=jax.ShapeDtypeStruct((b, l, hw), BF16),
        compiler_params=_params("arbitrary", "arbitrary"),
        name="attn_prompt",
    )(i_tab, j_tab, q, k, vt)


def _attn_sample_kernel(q_ref, k_ref, v_ref, o_ref, *, n_valid):
    s = lax.dot_general(q_ref[0], k_ref[0], _NT, preferred_element_type=F32)
    col = lax.broadcasted_iota(jnp.int32, s.shape, 1)
    s = jnp.where(col < n_valid, s, MASKED)
    p = jnp.exp2(s - jnp.max(s, axis=-1, keepdims=True))
    acc = jnp.dot(p.astype(BF16), v_ref[0], preferred_element_type=F32)
    o_ref[0] = (acc / acc[:, 0:1]).astype(BF16)


def _attn_sample(q, k, v, n_valid):
    b, t, hw = q.shape
    lk = k.shape[1]
    qmap = lambda bi, hi: (bi, 0, hi)
    return pl.pallas_call(
        functools.partial(_attn_sample_kernel, n_valid=n_valid),
        grid=(b, MLA_HEADS),
        in_specs=[pl.BlockSpec((1, t, HEAD_PAD), qmap),
                  pl.BlockSpec((1, lk, HEAD_PAD), qmap),
                  pl.BlockSpec((1, lk, HEAD_PAD), qmap)],
        out_specs=pl.BlockSpec((1, t, HEAD_PAD), qmap),
        out_shape=jax.ShapeDtypeStruct((b, t, hw), BF16),
        compiler_params=_params("arbitrary", "arbitrary"),
        name="attn_sample",
    )(q, k, v)


def _gla_kernel(gq_ref, gk_ref, gv_ref, lg_ref, og_ref, tri_ref, ggla_ref, st0_ref,
                o_ref, stf_ref, st_scr, b_scr, *, tg):
    li = pl.program_id(1)

    @pl.when(li == 0)
    def _():
        st_scr[...] = st0_ref[0]

    lg = lg_ref[0]
    lg_hi = lg.astype(BF16)
    lg_lo = (lg - lg_hi.astype(F32)).astype(BF16)
    tri = tri_ref[...]
    b_scr[...] = (jnp.dot(tri, lg_hi, preferred_element_type=F32)
                  + jnp.dot(tri, lg_lo, preferred_element_type=F32))

    lane = lax.broadcasted_iota(jnp.int32, (1, GLA_K_ALL), 1)
    head_of_lane = lane >> CHUNK_SHIFT
    ri = lax.broadcasted_iota(jnp.int32, (GLA_HEADS * CHUNK, CHUNK), 0) & (CHUNK - 1)
    cj = lax.broadcasted_iota(jnp.int32, (GLA_HEADS * CHUNK, CHUNK), 1)
    causal = cj <= ri
    g_out = ggla_ref[...]

    for c in range(tg // CHUNK):
        r0 = c * CHUNK
        b = b_scr[r0:r0 + CHUNK, :]
        b_mid = b[CHUNK // 2 - 1:CHUNK // 2, :]
        b_last = b[CHUNK - 1:CHUNK, :]
        q = gq_ref[0, r0:r0 + CHUNK, :].astype(F32)
        k = gk_ref[0, r0:r0 + CHUNK, :].astype(F32)
        qe = q * jnp.exp(jnp.minimum(b - b_mid, EXP_CLAMP))
        ke = (k * jnp.exp(jnp.minimum(b_mid - b, EXP_CLAMP))).astype(BF16)
        qb = q * jnp.exp(b)
        kd = k * jnp.exp(b_last - b)
        zero = jnp.zeros_like(q)
        qe_st = jnp.concatenate([jnp.where(head_of_lane == hd, qe, zero) for hd in range(GLA_HEADS)],
                                axis=0).astype(BF16)
        qb_st = jnp.concatenate([jnp.where(head_of_lane == hd, qb, zero) for hd in range(GLA_HEADS)],
                                axis=0).astype(BF16)
        a_st = lax.dot_general(qe_st, ke, _NT, preferred_element_type=F32)
        a_st = jnp.where(causal, a_st, 0.0).astype(BF16)
        st = st_scr[...]
        o_inter = lax.dot_general(qb_st, st.astype(BF16), _NT, preferred_element_type=F32)
        upd = jnp.zeros_like(st)
        for hd in range(GLA_HEADS):
            vh = gv_ref[0, r0:r0 + CHUNK, hd * GLA_DV:(hd + 1) * GLA_DV]
            o = (o_inter[hd * CHUNK:(hd + 1) * CHUNK, :]
                 + jnp.dot(a_st[hd * CHUNK:(hd + 1) * CHUNK, :], vh, preferred_element_type=F32))
            og = og_ref[0, r0:r0 + CHUNK, hd * GLA_DV:(hd + 1) * GLA_DV].astype(F32)
            o = _rms_rows(o) * g_out * (og * jax.nn.sigmoid(og))
            o_ref[0, r0:r0 + CHUNK, hd * GLA_DV:(hd + 1) * GLA_DV] = o.astype(BF16)
            kd_h = jnp.where(head_of_lane == hd, kd, zero).astype(BF16)
            upd = upd + lax.dot_general(vh, kd_h, _TN, preferred_element_type=F32)
        st_scr[...] = st * jnp.exp(b_last) + upd

    stf_ref[0] = st_scr[...]


def _gla(gq, gk, gv, lg, og, st0, w, tg):
    b, l, _ = gq.shape
    row = lambda bi, li: (bi, li, 0)
    fix = lambda bi, li: (bi, 0, 0)
    blk = jnp.arange(tg) // CHUNK
    tri = ((blk[:, None] == blk[None, :]) & (jnp.arange(tg)[None, :] <= jnp.arange(tg)[:, None])).astype(BF16)
    return pl.pallas_call(
        functools.partial(_gla_kernel, tg=tg),
        grid=(b, l // tg),
        in_specs=[pl.BlockSpec((1, tg, GLA_K_ALL), row),
                  pl.BlockSpec((1, tg, GLA_K_ALL), row),
                  pl.BlockSpec((1, tg, GLA_V_ALL), row),
                  pl.BlockSpec((1, tg, GLA_K_ALL), row),
                  pl.BlockSpec((1, tg, GLA_V_ALL), row),
                  _const_spec((tg, tg)),
                  _const_spec((1, GLA_DV)),
                  pl.BlockSpec((1, GLA_DV, GLA_K_ALL), fix)],
        out_specs=[pl.BlockSpec((1, tg, GLA_V_ALL), row),
                   pl.BlockSpec((1, GLA_DV, GLA_K_ALL), fix)],
        out_shape=[jax.ShapeDtypeStruct((b, l, GLA_V_ALL), BF16),
                   jax.ShapeDtypeStruct((b, GLA_DV, GLA_K_ALL), F32)],
        scratch_shapes=[pltpu.VMEM((GLA_DV, GLA_K_ALL), F32), pltpu.VMEM((tg, GLA_K_ALL), F32)],
        compiler_params=_params("arbitrary", "arbitrary"),
        name="gla",
    )(gq, gk, gv, lg, og, tri, w["g_gla"], st0)


def _ffn_kernel(x_ref, om_ref, ogl_ref, mods_ref, g2_ref, wom_ref, wog_ref, wup_ref, wcv_ref, bcv_ref,
                wdn_ref, hist_ref, y_ref, nh_ref, carry_scr, a_scr, acc_scr, *, tm):
    li = pl.program_id(1)
    hist_rows = CONV_W - 1
    pad = 8

    @pl.when(li == 0)
    def _():
        carry_scr[...] = jnp.zeros_like(carry_scr)
        carry_scr[pad - hist_rows:pad, :] = hist_ref[0]

    mixed = (jnp.dot(om_ref[0], wom_ref[...], preferred_element_type=F32)
             + jnp.dot(ogl_ref[0], wog_ref[...], preferred_element_type=F32))
    x1 = x_ref[0] + mods_ref[0, 2:3, :] * mixed
    h = (_rms_rows(x1) * g2_ref[...] * (1.0 + mods_ref[0, 4:5, :]) + mods_ref[0, 3:4, :]).astype(BF16)

    for f in range(FFN_DIM // FFN_TILE):
        c0 = f * FFN_TILE
        a = jnp.dot(h, wup_ref[:, c0:c0 + FFN_TILE], preferred_element_type=F32)
        g = jnp.dot(h, wup_ref[:, FFN_DIM + c0:FFN_DIM + c0 + FFN_TILE], preferred_element_type=F32)
        a_scr[0:pad, :] = carry_scr[:, c0:c0 + FFN_TILE]
        a_scr[pad:pad + tm, :] = a
        carry_scr[:, c0:c0 + FFN_TILE] = a[tm - pad:tm, :]
        nh_ref[0, :, c0:c0 + FFN_TILE] = a[tm - hist_rows:tm, :]
        conv = (bcv_ref[:, c0:c0 + FFN_TILE]
                + wcv_ref[2:3, c0:c0 + FFN_TILE] * a
                + wcv_ref[1:2, c0:c0 + FFN_TILE] * a_scr[pad - 1:pad - 1 + tm, :]
                + wcv_ref[0:1, c0:c0 + FFN_TILE] * a_scr[pad - 2:pad - 2 + tm, :])
        act = (jax.nn.gelu(conv) * g).astype(BF16)
        y = jnp.dot(act, wdn_ref[c0:c0 + FFN_TILE, :], preferred_element_type=F32)
        if f == 0:
            acc_scr[...] = y
        else:
            acc_scr[...] += y

    y_ref[0] = x1 + mods_ref[0, 5:6, :] * acc_scr[...]


def _ffn(x, o_mla, o_gla, mods, hist, w, tm):
    b, l, d = x.shape
    row = lambda bi, li: (bi, li, 0)
    fix = lambda bi, li: (bi, 0, 0)
    hw = MLA_HEADS * HEAD_PAD
    return pl.pallas_call(
        functools.partial(_ffn_kernel, tm=tm),
        grid=(b, l // tm),
        in_specs=[pl.BlockSpec((1, tm, d), row),
                  pl.BlockSpec((1, tm, hw), row),
                  pl.BlockSpec((1, tm, GLA_V_ALL), row),
                  pl.BlockSpec((1, 6, d), fix),
                  _const_spec((1, d)),
                  _const_spec((hw, d)),
                  _const_spec((GLA_V_ALL, d)),
                  _const_spec((d, 2 * FFN_DIM)),
                  _const_spec((CONV_W, FFN_DIM)),
                  _const_spec((1, FFN_DIM)),
                  _const_spec((FFN_DIM, d)),
                  pl.BlockSpec((1, CONV_W - 1, FFN_DIM), fix)],
        out_specs=[pl.BlockSpec((1, tm, d), row),
                   pl.BlockSpec((1, CONV_W - 1, FFN_DIM), fix)],
        out_shape=[jax.ShapeDtypeStruct((b, l, d), F32),
                   jax.ShapeDtypeStruct((b, CONV_W - 1, FFN_DIM), F32)],
        scratch_shapes=[pltpu.VMEM((8, FFN_DIM), F32),
                        pltpu.VMEM((tm + 8, FFN_TILE), F32),
                        pltpu.VMEM((tm, d), F32)],
        compiler_params=_params("arbitrary", "arbitrary"),
        name="ffn",
    )(x, o_mla, o_gla, mods, w["g_norm2"], w["w_out_mla"], w["w_out_gla"], w["w_up"], w["w_conv"],
      w["b_conv"], w["w_down"], hist)


def _prep_weights(w_in, g_norm1, g_qa, w_uq, g_qn, g_kva, w_ukv, g_kn, w_a2, b_a2, g_gla, w_out,
                  g_norm2, w_up, w_conv, b_conv, w_down):
    o, cols = 0, []
    for n in (MLA_Q_RANK, MLA_KV_RANK, MLA_ROPE, GLA_K_ALL, GLA_K_ALL, GLA_V_ALL, GLA_GATE_RANK, GLA_V_ALL):
        cols.append(w_in[:, o:o + n])
        o += n
    q_lat, kv_lat, kpe, gq, gk, gv, g_lr, og = cols
    misc_pad = jnp.zeros((D_MODEL, LANES - MLA_ROPE - GLA_GATE_RANK), w_in.dtype)
    w_in_r = jnp.concatenate([q_lat, kv_lat, gq, gk, gv, og, kpe, g_lr, misc_pad], axis=1).astype(BF16)

    wq = w_uq.reshape(MLA_Q_RANK, MLA_HEADS, MLA_QK)
    n_, r1, r2 = wq[..., :MLA_NOPE], wq[..., MLA_NOPE:MLA_NOPE + HALF_ROPE], wq[..., MLA_NOPE + HALF_ROPE:]
    w_uq_p = jnp.concatenate([n_, r1, r2, r2, r1], axis=-1).reshape(MLA_Q_RANK, MLA_HEADS * HEAD_PAD).astype(BF16)

    sc = MLA_QK ** -0.5 * math.log2(math.e)
    gn, g1, g2 = g_qn[:MLA_NOPE], g_qn[MLA_NOPE:MLA_NOPE + HALF_ROPE], g_qn[MLA_NOPE + HALF_ROPE:]
    z16, z64 = jnp.zeros((HALF_ROPE,), F32), jnp.zeros((MLA_NOPE,), F32)
    gq3 = jnp.stack([jnp.concatenate([gn, z16, z16, z16, z16]),
                     jnp.concatenate([z64, g1, g2, z16, z16]),
                     jnp.concatenate([z64, z16, z16, g2, g1])]) * sc
    kn, k1, k2 = g_kn[:MLA_NOPE], g_kn[MLA_NOPE:MLA_NOPE + HALF_ROPE], g_kn[MLA_NOPE + HALF_ROPE:]
    gk3 = jnp.stack([jnp.concatenate([kn, z16, z16, z16, z16]),
                     jnp.concatenate([z64, k1, k2, -k1, k2]),
                     jnp.concatenate([z64, -k2, k1, k2, k1])])
    eye = jnp.eye(HALF_ROPE, dtype=F32)
    zz = jnp.zeros((HALF_ROPE, HALF_ROPE), F32)
    pick1 = jnp.concatenate([eye, zz], axis=0)
    pick2 = jnp.concatenate([zz, eye], axis=0)
    z_n = jnp.zeros((MLA_ROPE, MLA_NOPE), F32)
    sel = jnp.concatenate([z_n, pick1, pick2, pick1, pick2, z_n, pick2, pick1, pick2, pick1], axis=1).astype(BF16)

    w_a2_p = jnp.zeros((LANES, GLA_K_ALL), F32).at[MLA_ROPE:MLA_ROPE + GLA_GATE_RANK].set(w_a2).astype(BF16)

    wo = w_out[:MLA_HEADS * MLA_V].reshape(MLA_HEADS, MLA_V, D_MODEL)
    w_out_mla = jnp.concatenate([jnp.zeros_like(wo), wo], axis=1).reshape(MLA_HEADS * HEAD_PAD, D_MODEL).astype(BF16)
    return {
        "g_norm1": g_norm1.reshape(1, -1), "w_in": w_in_r, "g_qa": g_qa.reshape(1, -1), "w_uq": w_uq_p,
        "gq3": gq3, "g_kva": g_kva.reshape(1, -1), "w_ukv": w_ukv.astype(BF16), "sel": sel, "gk3": gk3,
        "w_a2": w_a2_p, "b_a2": b_a2.reshape(1, -1), "g_gla": g_gla.reshape(1, -1),
        "w_out_mla": w_out_mla, "w_out_gla": w_out[MLA_HEADS * MLA_V:].astype(BF16),
        "g_norm2": g_norm2.reshape(1, -1), "w_up": w_up.astype(BF16), "w_conv": w_conv,
        "b_conv": b_conv.reshape(1, -1), "w_down": w_down.astype(BF16),
    }


def _rope_tables(n):
    inv = 1.0 / (ROPE_THETA ** (jnp.arange(HALF_ROPE, dtype=F32) / HALF_ROPE))
    ang = jnp.arange(n, dtype=F32)[:, None] * jnp.tile(inv, LANES // HALF_ROPE)[None, :]
    return jnp.cos(ang), jnp.sin(ang)


def _state_to_t(s):
    b = s.shape[0]
    return s.reshape(b, GLA_K_ALL, GLA_DV).transpose(0, 2, 1)


def _state_from_t(st):
    b = st.shape[0]
    return st.transpose(0, 2, 1).reshape(b, GLA_HEADS, GLA_DK, GLA_DV)


def _pick_tile(n, pref):
    t = min(n, pref)
    while n % t:
        t //= 2
    return t


def _layer(x, mods, w, cos_t, sin_t, cache_ckv, cache_kpe, state, hist):
    b, l, _ = x.shape
    tm = _pick_tile(l, 512)
    pos0 = 0 if cache_ckv is None else cache_ckv.shape[1]
    ckv, kpe, q, gq, gk, gv, lg, og = _inproj(x, mods, w, cos_t, sin_t, tm, pos0)
    if cache_ckv is None:
        k, vt = _keys(ckv, kpe, w, cos_t, sin_t, tm, True)
        o_mla = _attn_prompt(q, k, vt, tm)
    else:
        n_valid = pos0 + l
        lk = -(-n_valid // LANES) * LANES
        ckv_all = jnp.concatenate([cache_ckv, ckv, jnp.zeros((b, lk - n_valid, MLA_KV_RANK), F32)], axis=1)
        kpe_all = jnp.concatenate([cache_kpe, kpe, jnp.zeros((b, lk - n_valid, MLA_ROPE), F32)], axis=1)
        k, v = _keys(ckv_all, kpe_all, w, cos_t, sin_t, lk, False)
        o_mla = _attn_sample(q, k, v, n_valid)
    o_gla, st = _gla(gq, gk, gv, lg, og, _state_to_t(state), w, tm)
    y, new_hist = _ffn(x, o_mla, o_gla, mods, hist, w, tm)
    return y, ckv, kpe, _state_from_t(st), new_hist


def kernel(x_prompt, x_sample, c_prompt, c_sample, cache_ckv, cache_kpe, state_gla, state_ffn_conv, w_ada, b_ada, g_norm1, w_in, g_qa, w_uq, g_qn, g_kva, w_ukv, g_kn, w_a2, b_a2, g_gla, w_out, g_norm2, w_up, w_conv, b_conv, w_down):
    depth = w_ada.shape[0]
    bp, lp, _ = x_prompt.shape
    bs, ls, _ = x_sample.shape
    n_pos = max(lp, cache_ckv.shape[2] + ls)
    n_pos = -(-n_pos // LANES) * LANES
    cos_t, sin_t = _rope_tables(n_pos)
    yp, ys = x_prompt, x_sample
    outs = [[] for _ in range(8)]
    for i in range(depth):
        w = _prep_weights(w_in[i], g_norm1[i], g_qa[i], w_uq[i], g_qn[i], g_kva[i], w_ukv[i], g_kn[i],
                          w_a2[i], b_a2[i], g_gla[i], w_out[i], g_norm2[i], w_up[i], w_conv[i], b_conv[i],
                          w_down[i])
        mods = _adaln(jnp.concatenate([c_prompt, c_sample], axis=0), w_ada[i], b_ada[i])
        mods = mods.reshape(bp + bs, 6, D_MODEL)
        zero_state = jnp.zeros((bp, GLA_HEADS, GLA_DK, GLA_DV), F32)
        zero_hist = jnp.zeros((bp, CONV_W - 1, FFN_DIM), F32)
        yp, a, b_, s, h = _layer(yp, mods[:bp], w, cos_t, sin_t, None, None, zero_state, zero_hist)
        for lst, val in zip(outs[:4], (a, b_, s, h)):
            lst.append(val)
        ys, a, b_, s, h = _layer(ys, mods[bp:], w, cos_t, sin_t, cache_ckv[i], cache_kpe[i],
                                 state_gla[i], state_ffn_conv[i])
        for lst, val in zip(outs[4:], (a, b_, s, h)):
            lst.append(val)
    return (yp, ys) + tuple(jnp.stack(o) for o in outs)
```

```python
import functools
import math

import jax
import jax.numpy as jnp
from jax import lax
from jax.experimental import pallas as pl
from jax.experimental.pallas import tpu as pltpu

F32 = jnp.float32
BF16 = jnp.bfloat16

D_MODEL = 1024
CHUNK = 64
CHUNK_SHIFT = 6
MLA_HEADS = 8
MLA_NOPE = 64
MLA_ROPE = 32
MLA_QK = MLA_NOPE + MLA_ROPE
MLA_V = 64
MLA_Q_RANK = 384
MLA_KV_RANK = 256
ROPE_THETA = 10000.0
GLA_HEADS = 4
GLA_DK = 64
GLA_DV = 128
GLA_GATE_RANK = 16
GLA_TAU = 16.0
FFN_DIM = 2816
CONV_W = 3
EPS = 1e-6

LANES = 128
HEAD_PAD = LANES
HALF_ROPE = MLA_ROPE // 2
GLA_K_ALL = GLA_HEADS * GLA_DK
GLA_V_ALL = GLA_HEADS * GLA_DV
C_QLAT = 0
C_KV = C_QLAT + MLA_Q_RANK
C_GQ = C_KV + MLA_KV_RANK
C_GK = C_GQ + GLA_K_ALL
C_GV = C_GK + GLA_K_ALL
C_OG = C_GV + GLA_V_ALL
C_MISC = C_OG + GLA_V_ALL
IN_COLS_PAD = C_MISC + LANES
FFN_TILE = 256
EXP_CLAMP = 80.0
VMEM_LIMIT = 56 * 1024 * 1024


def _const_spec(shape):
    nd = len(shape)
    return pl.BlockSpec(shape, lambda *_: (0,) * nd, pipeline_mode=pl.Buffered(1))


def _params(*sem):
    return pltpu.CompilerParams(dimension_semantics=sem, vmem_limit_bytes=VMEM_LIMIT)


def _rms_rows(x):
    return x * lax.rsqrt(jnp.mean(x * x, axis=-1, keepdims=True) + EPS)


def _ada_kernel(c_ref, w_ref, b_ref, o_ref):
    c = c_ref[...]
    s = c * jax.nn.sigmoid(c)
    o_ref[...] = jnp.dot(s, w_ref[...], precision=lax.Precision.HIGHEST,
                         preferred_element_type=F32) + b_ref[...]


def _adaln(c, w_ada, b_ada):
    n, d = c.shape
    cols = w_ada.shape[1]
    tn = 1024
    return pl.pallas_call(
        _ada_kernel,
        grid=(cols // tn,),
        in_specs=[pl.BlockSpec((n, d), lambda j: (0, 0)),
                  pl.BlockSpec((d, tn), lambda j: (0, j)),
                  pl.BlockSpec((1, tn), lambda j: (0, j))],
        out_specs=pl.BlockSpec((n, tn), lambda j: (0, j)),
        out_shape=jax.ShapeDtypeStruct((n, cols), F32),
        compiler_params=_params("arbitrary"),
        name="adaln",
    )(c, w_ada, b_ada.reshape(1, cols))


def _inproj_kernel(x_ref, mods_ref, g1_ref, win_ref, gqa_ref, wuq_ref, gq3_ref, cos_ref, sin_ref,
                   gkva_ref, wa2_ref, ba2_ref,
                   ckv_ref, kpe_ref, q_ref, gq_ref, gk_ref, gv_ref, lg_ref, og_ref):
    x = x_ref[0]
    shift = mods_ref[0, 0:1, :]
    scale = mods_ref[0, 1:2, :]
    h = _rms_rows(x) * g1_ref[...] * (1.0 + scale) + shift
    p = jnp.dot(h.astype(BF16), win_ref[...], preferred_element_type=F32)

    ckv_ref[0] = _rms_rows(p[:, C_KV:C_GQ]) * gkva_ref[...]
    misc = p[:, C_MISC:IN_COLS_PAD]
    kpe_ref[0] = misc[:, 0:MLA_ROPE]

    qa = _rms_rows(p[:, C_QLAT:C_KV]) * gqa_ref[...]
    qu = jnp.dot(qa.astype(BF16), wuq_ref[...], preferred_element_type=F32)
    tq = gq3_ref[0:1, :] + cos_ref[...] * gq3_ref[1:2, :] + sin_ref[...] * gq3_ref[2:3, :]
    lane = lax.broadcasted_iota(jnp.int32, (1, HEAD_PAD), 1)
    real = (lane < MLA_QK).astype(F32)
    for hd in range(MLA_HEADS):
        xh = qu[:, hd * HEAD_PAD:(hd + 1) * HEAD_PAD]
        ss = jnp.sum(xh * xh * real, axis=-1, keepdims=True)
        r = lax.rsqrt(ss * (1.0 / MLA_QK) + EPS)
        q_ref[0, :, hd * HEAD_PAD:(hd + 1) * HEAD_PAD] = (xh * r * tq).astype(BF16)

    gq_ref[0] = (p[:, C_GQ:C_GK] * (GLA_DK ** -0.5)).astype(BF16)
    gk_ref[0] = p[:, C_GK:C_GV].astype(BF16)
    gv_ref[0] = p[:, C_GV:C_OG].astype(BF16)
    og_ref[0] = p[:, C_OG:C_MISC].astype(BF16)
    z = jnp.dot(misc.astype(BF16), wa2_ref[...], preferred_element_type=F32) + ba2_ref[...]
    log_sig = jnp.minimum(z, 0.0) - jnp.log1p(jnp.exp(-jnp.abs(z)))
    lg_ref[0] = log_sig * (1.0 / GLA_TAU)


def _inproj(x, mods, w, cos_t, sin_t, tm, pos0):
    b, l, d = x.shape
    nt = l // tm
    p0 = pos0 // tm
    row = lambda bi, li: (bi, li, 0)
    tab = lambda bi, li: (li + p0, 0)
    outs = [(MLA_KV_RANK, F32), (MLA_ROPE, F32), (MLA_HEADS * HEAD_PAD, BF16), (GLA_K_ALL, BF16),
            (GLA_K_ALL, BF16), (GLA_V_ALL, BF16), (GLA_K_ALL, F32), (GLA_V_ALL, BF16)]
    return pl.pallas_call(
        _inproj_kernel,
        grid=(b, nt),
        in_specs=[pl.BlockSpec((1, tm, d), row),
                  pl.BlockSpec((1, 6, d), lambda bi, li: (bi, 0, 0)),
                  _const_spec((1, d)),
                  _const_spec((d, IN_COLS_PAD)),
                  _const_spec((1, MLA_Q_RANK)),
                  _const_spec((MLA_Q_RANK, MLA_HEADS * HEAD_PAD)),
                  _const_spec((3, HEAD_PAD)),
                  pl.BlockSpec((tm, HEAD_PAD), tab),
                  pl.BlockSpec((tm, HEAD_PAD), tab),
                  _const_spec((1, MLA_KV_RANK)),
                  _const_spec((LANES, GLA_K_ALL)),
                  _const_spec((1, GLA_K_ALL))],
        out_specs=[pl.BlockSpec((1, tm, n), row) for n, _ in outs],
        out_shape=[jax.ShapeDtypeStruct((b, l, n), dt) for n, dt in outs],
        compiler_params=_params("arbitrary", "arbitrary"),
        name="inproj",
    )(x, mods, w["g_norm1"], w["w_in"], w["g_qa"], w["w_uq"], w["gq3"], cos_t, sin_t,
      w["g_kva"], w["w_a2"], w["b_a2"])


def _keys_kernel(ckv_ref, kpe_ref, wukv_ref, sel_ref, gk3_ref, cos_ref, sin_ref, k_ref, v_ref, *,
                 transpose_v):
    kv = jnp.dot(ckv_ref[0].astype(BF16), wukv_ref[...], preferred_element_type=F32)
    kpe = kpe_ref[0]
    kpe_hi = kpe.astype(BF16)
    kpe_lo = (kpe - kpe_hi.astype(F32)).astype(BF16)
    uv = (jnp.dot(kpe_hi, sel_ref[...], preferred_element_type=F32)
          + jnp.dot(kpe_lo, sel_ref[...], preferred_element_type=F32))
    rot = (uv[:, :HEAD_PAD] * (cos_ref[...] * gk3_ref[1:2, :])
           + uv[:, HEAD_PAD:] * (sin_ref[...] * gk3_ref[2:3, :]))
    sp = jnp.sum(kpe * kpe, axis=-1, keepdims=True)
    lane = lax.broadcasted_iota(jnp.int32, (1, HEAD_PAD), 1)
    is_nope = lane < MLA_NOPE
    g_nope = gk3_ref[0:1, :]
    for hd in range(MLA_HEADS):
        blk = kv[:, hd * HEAD_PAD:(hd + 1) * HEAD_PAD]
        kn = jnp.where(is_nope, blk, 0.0)
        ss = jnp.sum(kn * kn, axis=-1, keepdims=True) + sp
        r = lax.rsqrt(ss * (1.0 / MLA_QK) + EPS)
        k_ref[0, :, hd * HEAD_PAD:(hd + 1) * HEAD_PAD] = ((kn * g_nope + rot) * r).astype(BF16)
        vp = jnp.where(is_nope, 1.0, blk)
        if transpose_v:
            v_ref[0, hd * HEAD_PAD:(hd + 1) * HEAD_PAD, :] = vp.T.astype(BF16)
        else:
            v_ref[0, :, hd * HEAD_PAD:(hd + 1) * HEAD_PAD] = vp.astype(BF16)


def _keys(ckv, kpe, w, cos_t, sin_t, tm, transpose_v):
    b, l, _ = ckv.shape
    row = lambda bi, li: (bi, li, 0)
    tab = lambda bi, li: (li, 0)
    hw = MLA_HEADS * HEAD_PAD
    if transpose_v:
        v_spec = pl.BlockSpec((1, hw, tm), lambda bi, li: (bi, 0, li))
        v_shape = jax.ShapeDtypeStruct((b, hw, l), BF16)
    else:
        v_spec = pl.BlockSpec((1, tm, hw), row)
        v_shape = jax.ShapeDtypeStruct((b, l, hw), BF16)
    return pl.pallas_call(
        functools.partial(_keys_kernel, transpose_v=transpose_v),
        grid=(b, l // tm),
        in_specs=[pl.BlockSpec((1, tm, MLA_KV_RANK), row),
                  pl.BlockSpec((1, tm, MLA_ROPE), row),
                  _const_spec((MLA_KV_RANK, hw)),
                  _const_spec((MLA_ROPE, 2 * HEAD_PAD)),
                  _const_spec((3, HEAD_PAD)),
                  pl.BlockSpec((tm, HEAD_PAD), tab),
                  pl.BlockSpec((tm, HEAD_PAD), tab)],
        out_specs=[pl.BlockSpec((1, tm, hw), row), v_spec],
        out_shape=[jax.ShapeDtypeStruct((b, l, hw), BF16), v_shape],
        compiler_params=_params("arbitrary", "arbitrary"),
        name="keys",
    )(ckv, kpe, w["w_ukv"], w["sel"], w["gk3"], cos_t, sin_t)


_NT = (((1,), (1,)), ((), ()))
_TN = (((0,), (0,)), ((), ()))
MASKED = -1e30
EXP2_SAFE = 100.0


def _finalize_tile(acc_scr, o_ref, i, tq):
    acc = acc_scr[...]
    o_ref[0, pl.ds(pl.multiple_of(i * tq, tq), tq), :] = (acc / acc[0:1, :]).T.astype(BF16)


def _chunk_visible(tq):
    kc = lax.broadcasted_iota(jnp.int32, (tq, tq), 0) >> CHUNK_SHIFT
    qc = lax.broadcasted_iota(jnp.int32, (tq, tq), 1) >> CHUNK_SHIFT
    return kc <= qc


ATTN_UNROLL = 4


def _attn_bounded_kernel(iq_tab, ia_tab, j_tab, d_tab, q_ref, k_ref, vt_ref, mask_ref, o_ref, p_buf, acc_scr,
                         *, tq, nq, n_iter):
    p_buf[...] = jnp.zeros_like(p_buf)
    acc_scr[...] = jnp.zeros_like(acc_scr)

    def sub_step(t, slot):
        t1 = jnp.maximum(t - 1, 0)
        j1 = j_tab[t1]
        vt = vt_ref[0, :, pl.ds(pl.multiple_of(j1 * tq, tq), tq)]
        cols = pl.ds(pl.multiple_of(ia_tab[t1] * tq, tq), tq)
        acc_scr[:, cols] = (jnp.where(j1 == 0, 0.0, acc_scr[:, cols])
                            + jnp.dot(vt, p_buf[1 - slot], preferred_element_type=F32))
        kb = k_ref[0, pl.ds(pl.multiple_of(j_tab[t] * tq, tq), tq), :]
        qt = q_ref[0, pl.ds(pl.multiple_of(iq_tab[t] * tq, tq), tq), :]
        p = jnp.exp2(lax.dot_general(kb, qt, _NT, preferred_element_type=F32)).astype(BF16)
        p_buf[slot] = p * mask_ref[d_tab[t]]

    def body(it, carry):
        for u in range(ATTN_UNROLL):
            sub_step(ATTN_UNROLL * it + u, u % 2)
        return carry

    lax.fori_loop(0, n_iter, body, 0)
    for i in range(nq):
        acc = acc_scr[:, i * tq:(i + 1) * tq]
        o_ref[0, i * tq:(i + 1) * tq, :] = (acc / acc[0:1, :]).T.astype(BF16)


def _attn_online_kernel(i_tab, j_tab, q_ref, k_ref, vt_ref, o_ref, s_buf, p_buf, a_buf, m_scr, acc_scr,
                        *, tq, n_steps, n_iter):
    s_buf[...] = jnp.zeros_like(s_buf)
    p_buf[...] = jnp.zeros_like(p_buf)
    a_buf[...] = jnp.zeros_like(a_buf)
    m_scr[...] = jnp.zeros_like(m_scr)
    acc_scr[...] = jnp.zeros_like(acc_scr)

    def sub_step(t, slot):
        other = 1 - slot
        t2 = jnp.maximum(t - 2, 0)
        i2, j2 = i_tab[t2], j_tab[t2]
        vt = vt_ref[0, :, pl.ds(pl.multiple_of(j2 * tq, tq), tq)]
        acc_scr[...] = a_buf[slot] * acc_scr[...] + jnp.dot(vt, p_buf[slot], preferred_element_type=F32)
        t1 = jnp.maximum(t - 1, 0)
        s = s_buf[other]
        m_old = jnp.where(j_tab[t1] == 0, -jnp.inf, m_scr[...])
        m_new = jnp.maximum(m_old, jnp.max(s, axis=0, keepdims=True))
        a_buf[other] = jnp.exp2(m_old - m_new)
        p_buf[other] = jnp.exp2(s - m_new).astype(BF16)
        m_scr[...] = m_new
        t0 = jnp.minimum(t, n_steps - 1)
        i0, j0 = i_tab[t0], j_tab[t0]
        kb = k_ref[0, pl.ds(pl.multiple_of(j0 * tq, tq), tq), :]
        qt = q_ref[0, pl.ds(pl.multiple_of(i0 * tq, tq), tq), :]
        s_buf[slot] = lax.dot_general(kb, qt, _NT, preferred_element_type=F32)

        @pl.when(i0 == j0)
        def _():
            s_buf[slot] = jnp.where(_chunk_visible(tq), s_buf[slot], MASKED)

        @pl.when(jnp.logical_and(jnp.logical_and(t >= 2, t - 2 < n_steps), i2 == j2))
        def _():
            _finalize_tile(acc_scr, o_ref, i2, tq)

    def body(it, carry):
        sub_step(2 * it, 0)
        sub_step(2 * it + 1, 1)
        return carry

    lax.fori_loop(0, n_iter, body, 0)


def _attn_specs(b, l, n_tabs, extra_in=()):
    qmap = lambda bi, hi, *_: (bi, 0, hi)
    return dict(
        num_scalar_prefetch=n_tabs,
        grid=(b, MLA_HEADS),
        in_specs=[pl.BlockSpec((1, l, HEAD_PAD), qmap),
                  pl.BlockSpec((1, l, HEAD_PAD), qmap),
                  pl.BlockSpec((1, HEAD_PAD, l), lambda bi, hi, *_: (bi, hi, 0)), *extra_in],
        out_specs=pl.BlockSpec((1, l, HEAD_PAD), qmap))


def _attn_online_call(q, k, vt, tq):
    b, l, hw = q.shape
    nq = l // tq
    pairs = [(i, j) for i in range(nq) for j in range(i + 1)]
    n_steps = len(pairs)
    n_iter = (n_steps + 3) // 2
    pairs = pairs + [pairs[-1]] * (2 * n_iter - n_steps)
    i_tab = jnp.asarray([p_[0] for p_ in pairs], jnp.int32)
    j_tab = jnp.asarray([p_[1] for p_ in pairs], jnp.int32)
    grid_spec = pltpu.PrefetchScalarGridSpec(
        **_attn_specs(b, l, 2),
        scratch_shapes=[pltpu.VMEM((2, tq, tq), F32), pltpu.VMEM((2, tq, tq), BF16),
                        pltpu.VMEM((2, 1, tq), F32), pltpu.VMEM((1, tq), F32),
                        pltpu.VMEM((HEAD_PAD, tq), F32)])
    return pl.pallas_call(
        functools.partial(_attn_online_kernel, tq=tq, n_steps=n_steps, n_iter=n_iter),
        grid_spec=grid_spec,
        out_shape=jax.ShapeDtypeStruct((b, l, hw), BF16),
        compiler_params=_params("arbitrary", "arbitrary"),
        name="attn_online",
    )(i_tab, j_tab, q, k, vt)


def _attn_bounded_call(q, k, vt, tq):
    b, l, hw = q.shape
    nq = l // tq
    pairs = [(i, j) for i in range(nq) for j in range(i + 1)]
    n_iter = -(-(len(pairs) + 1) // ATTN_UNROLL)
    n_dummy = n_iter * ATTN_UNROLL - len(pairs)
    iq_tab = jnp.asarray([p_[0] for p_ in pairs] + [nq - 1] * n_dummy, jnp.int32)
    ia_tab = jnp.asarray([p_[0] for p_ in pairs] + [nq] * n_dummy, jnp.int32)
    j_tab = jnp.asarray([p_[1] for p_ in pairs] + [0] * n_dummy, jnp.int32)
    d_tab = jnp.asarray([int(p_[0] == p_[1]) for p_ in pairs] + [0] * n_dummy, jnp.int32)
    masks = jnp.stack([jnp.ones((tq, tq), BF16), _chunk_visible(tq).astype(BF16)])
    grid_spec = pltpu.PrefetchScalarGridSpec(
        **_attn_specs(b, l, 4, [pl.BlockSpec((2, tq, tq), lambda *_: (0, 0, 0), pipeline_mode=pl.Buffered(1))]),
        scratch_shapes=[pltpu.VMEM((2, tq, tq), BF16), pltpu.VMEM((HEAD_PAD, (nq + 1) * tq), F32)])
    return pl.pallas_call(
        functools.partial(_attn_bounded_kernel, tq=tq, nq=nq, n_iter=n_iter),
        grid_spec=grid_spec,
        out_shape=jax.ShapeDtypeStruct((b, l, hw), BF16),
        compiler_params=_params("arbitrary", "arbitrary"),
        name="attn_bounded",
    )(iq_tab, ia_tab, j_tab, d_tab, q, k, vt, masks)


def _attn_prompt(q, k, vt, tq, score_bound):
    return lax.cond(score_bound <= EXP2_SAFE,
                    functools.partial(_attn_bounded_call, tq=tq),
                    functools.partial(_attn_online_call, tq=tq),
                    q, k, vt)


def _attn_sample_kernel(q_ref, k_ref, v_ref, o_ref, *, n_valid):
    s = lax.dot_general(q_ref[0], k_ref[0], _NT, preferred_element_type=F32)
    col = lax.broadcasted_iota(jnp.int32, s.shape, 1)
    s = jnp.where(col < n_valid, s, MASKED)
    p = jnp.exp2(s - jnp.max(s, axis=-1, keepdims=True))
    acc = jnp.dot(p.astype(BF16), v_ref[0], preferred_element_type=F32)
    o_ref[0] = (acc / acc[:, 0:1]).astype(BF16)


def _attn_sample(q, k, v, n_valid):
    b, t, hw = q.shape
    lk = k.shape[1]
    qmap = lambda bi, hi: (bi, 0, hi)
    return pl.pallas_call(
        functools.partial(_attn_sample_kernel, n_valid=n_valid),
        grid=(b, MLA_HEADS),
        in_specs=[pl.BlockSpec((1, t, HEAD_PAD), qmap),
                  pl.BlockSpec((1, lk, HEAD_PAD), qmap),
                  pl.BlockSpec((1, lk, HEAD_PAD), qmap)],
        out_specs=pl.BlockSpec((1, t, HEAD_PAD), qmap),
        out_shape=jax.ShapeDtypeStruct((b, t, hw), BF16),
        compiler_params=_params("arbitrary", "arbitrary"),
        name="attn_sample",
    )(q, k, v)


def _gla_kernel(gq_ref, gk_ref, gv_ref, lg_ref, og_ref, tri_ref, ggla_ref, st0_ref,
                o_ref, stf_ref, st_scr, b_scr, *, tg):
    li = pl.program_id(1)

    @pl.when(li == 0)
    def _():
        st_scr[...] = st0_ref[0]

    lg = lg_ref[0]
    lg_hi = lg.astype(BF16)
    lg_lo = (lg - lg_hi.astype(F32)).astype(BF16)
    tri = tri_ref[...]
    b_scr[...] = (jnp.dot(tri, lg_hi, preferred_element_type=F32)
                  + jnp.dot(tri, lg_lo, preferred_element_type=F32))

    lane = lax.broadcasted_iota(jnp.int32, (1, GLA_K_ALL), 1)
    head_of_lane = lane >> CHUNK_SHIFT
    ri = lax.broadcasted_iota(jnp.int32, (GLA_HEADS * CHUNK, CHUNK), 0) & (CHUNK - 1)
    cj = lax.broadcasted_iota(jnp.int32, (GLA_HEADS * CHUNK, CHUNK), 1)
    causal = cj <= ri
    g_out = ggla_ref[...]

    for c in range(tg // CHUNK):
        r0 = c * CHUNK
        b = b_scr[r0:r0 + CHUNK, :]
        b_mid = b[CHUNK // 2 - 1:CHUNK // 2, :]
        b_last = b[CHUNK - 1:CHUNK, :]
        q = gq_ref[0, r0:r0 + CHUNK, :].astype(F32)
        k = gk_ref[0, r0:r0 + CHUNK, :].astype(F32)
        qe = q * jnp.exp(jnp.minimum(b - b_mid, EXP_CLAMP))
        ke = (k * jnp.exp(jnp.minimum(b_mid - b, EXP_CLAMP))).astype(BF16)
        qb = q * jnp.exp(b)
        kd = k * jnp.exp(b_last - b)
        zero = jnp.zeros_like(q)
        qe_st = jnp.concatenate([jnp.where(head_of_lane == hd, qe, zero) for hd in range(GLA_HEADS)],
                                axis=0).astype(BF16)
        qb_st = jnp.concatenate([jnp.where(head_of_lane == hd, qb, zero) for hd in range(GLA_HEADS)],
                                axis=0).astype(BF16)
        a_st = lax.dot_general(qe_st, ke, _NT, preferred_element_type=F32)
        a_st = jnp.where(causal, a_st, 0.0).astype(BF16)
        st = st_scr[...]
        o_inter = lax.dot_general(qb_st, st.astype(BF16), _NT, preferred_element_type=F32)
        upd = jnp.zeros_like(st)
        for hd in range(GLA_HEADS):
            vh = gv_ref[0, r0:r0 + CHUNK, hd * GLA_DV:(hd + 1) * GLA_DV]
            o = (o_inter[hd * CHUNK:(hd + 1) * CHUNK, :]
                 + jnp.dot(a_st[hd * CHUNK:(hd + 1) * CHUNK, :], vh, preferred_element_type=F32))
            og = og_ref[0, r0:r0 + CHUNK, hd * GLA_DV:(hd + 1) * GLA_DV].astype(F32)
            o = _rms_rows(o) * g_out * (og * jax.nn.sigmoid(og))
            o_ref[0, r0:r0 + CHUNK, hd * GLA_DV:(hd + 1) * GLA_DV] = o.astype(BF16)
            kd_h = jnp.where(head_of_lane == hd, kd, zero).astype(BF16)
            upd = upd + lax.dot_general(vh, kd_h, _TN, preferred_element_type=F32)
        st_scr[...] = st * jnp.exp(b_last) + upd

    stf_ref[0] = st_scr[...]


def _gla(gq, gk, gv, lg, og, st0, w, tg):
    b, l, _ = gq.shape
    row = lambda bi, li: (bi, li, 0)
    fix = lambda bi, li: (bi, 0, 0)
    blk = jnp.arange(tg) // CHUNK
    tri = ((blk[:, None] == blk[None, :]) & (jnp.arange(tg)[None, :] <= jnp.arange(tg)[:, None])).astype(BF16)
    return pl.pallas_call(
        functools.partial(_gla_kernel, tg=tg),
        grid=(b, l // tg),
        in_specs=[pl.BlockSpec((1, tg, GLA_K_ALL), row),
                  pl.BlockSpec((1, tg, GLA_K_ALL), row),
                  pl.BlockSpec((1, tg, GLA_V_ALL), row),
                  pl.BlockSpec((1, tg, GLA_K_ALL), row),
                  pl.BlockSpec((1, tg, GLA_V_ALL), row),
                  _const_spec((tg, tg)),
                  _const_spec((1, GLA_DV)),
                  pl.BlockSpec((1, GLA_DV, GLA_K_ALL), fix)],
        out_specs=[pl.BlockSpec((1, tg, GLA_V_ALL), row),
                   pl.BlockSpec((1, GLA_DV, GLA_K_ALL), fix)],
        out_shape=[jax.ShapeDtypeStruct((b, l, GLA_V_ALL), BF16),
                   jax.ShapeDtypeStruct((b, GLA_DV, GLA_K_ALL), F32)],
        scratch_shapes=[pltpu.VMEM((GLA_DV, GLA_K_ALL), F32), pltpu.VMEM((tg, GLA_K_ALL), F32)],
        compiler_params=_params("arbitrary", "arbitrary"),
        name="gla",
    )(gq, gk, gv, lg, og, tri, w["g_gla"], st0)


def _ffn_kernel(x_ref, om_ref, ogl_ref, mods_ref, g2_ref, wom_ref, wog_ref, wup_ref, wcv_ref, bcv_ref,
                wdn_ref, hist_ref, y_ref, nh_ref, carry_scr, a_scr, act_scr, *, tm):
    li = pl.program_id(1)
    hist_rows = CONV_W - 1
    pad = 8

    @pl.when(li == 0)
    def _():
        carry_scr[...] = jnp.zeros_like(carry_scr)
        carry_scr[pad - hist_rows:pad, :] = hist_ref[0]

    mixed = (jnp.dot(om_ref[0], wom_ref[...], preferred_element_type=F32)
             + jnp.dot(ogl_ref[0], wog_ref[...], preferred_element_type=F32))
    x1 = x_ref[0] + mods_ref[0, 2:3, :] * mixed
    h = (_rms_rows(x1) * g2_ref[...] * (1.0 + mods_ref[0, 4:5, :]) + mods_ref[0, 3:4, :]).astype(BF16)

    for f in range(FFN_DIM // FFN_TILE):
        c0 = f * FFN_TILE
        a = jnp.dot(h, wup_ref[:, c0:c0 + FFN_TILE], preferred_element_type=F32)
        g = jnp.dot(h, wup_ref[:, FFN_DIM + c0:FFN_DIM + c0 + FFN_TILE], preferred_element_type=F32)
        a_scr[0:pad, :] = carry_scr[:, c0:c0 + FFN_TILE]
        a_scr[pad:pad + tm, :] = a
        carry_scr[:, c0:c0 + FFN_TILE] = a[tm - pad:tm, :]
        nh_ref[0, :, c0:c0 + FFN_TILE] = a[tm - hist_rows:tm, :]
        conv = (bcv_ref[:, c0:c0 + FFN_TILE]
                + wcv_ref[2:3, c0:c0 + FFN_TILE] * a
                + wcv_ref[1:2, c0:c0 + FFN_TILE] * a_scr[pad - 1:pad - 1 + tm, :]
                + wcv_ref[0:1, c0:c0 + FFN_TILE] * a_scr[pad - 2:pad - 2 + tm, :])
        act = (jax.nn.gelu(conv) * g).astype(BF16)
        act_scr[:, c0:c0 + FFN_TILE] = act

    y = jnp.dot(act_scr[...], wdn_ref[...], preferred_element_type=F32)
    y_ref[0] = x1 + mods_ref[0, 5:6, :] * y


def _ffn(x, o_mla, o_gla, mods, hist, w, tm):
    b, l, d = x.shape
    row = lambda bi, li: (bi, li, 0)
    fix = lambda bi, li: (bi, 0, 0)
    hw = MLA_HEADS * HEAD_PAD
    return pl.pallas_call(
        functools.partial(_ffn_kernel, tm=tm),
        grid=(b, l // tm),
        in_specs=[pl.BlockSpec((1, tm, d), row),
                  pl.BlockSpec((1, tm, hw), row),
                  pl.BlockSpec((1, tm, GLA_V_ALL), row),
                  pl.BlockSpec((1, 6, d), fix),
                  _const_spec((1, d)),
                  _const_spec((hw, d)),
                  _const_spec((GLA_V_ALL, d)),
                  _const_spec((d, 2 * FFN_DIM)),
                  _const_spec((CONV_W, FFN_DIM)),
                  _const_spec((1, FFN_DIM)),
                  _const_spec((FFN_DIM, d)),
                  pl.BlockSpec((1, CONV_W - 1, FFN_DIM), fix)],
        out_specs=[pl.BlockSpec((1, tm, d), row),
                   pl.BlockSpec((1, CONV_W - 1, FFN_DIM), fix)],
        out_shape=[jax.ShapeDtypeStruct((b, l, d), F32),
                   jax.ShapeDtypeStruct((b, CONV_W - 1, FFN_DIM), F32)],
        scratch_shapes=[pltpu.VMEM((8, FFN_DIM), F32),
                        pltpu.VMEM((tm + 8, FFN_TILE), F32),
                        pltpu.VMEM((tm, FFN_DIM), BF16)],
        compiler_params=_params("arbitrary", "arbitrary"),
        name="ffn",
    )(x, o_mla, o_gla, mods, w["g_norm2"], w["w_out_mla"], w["w_out_gla"], w["w_up"], w["w_conv"],
      w["b_conv"], w["w_down"], hist)


def _prep_weights(w_in, g_norm1, g_qa, w_uq, g_qn, g_kva, w_ukv, g_kn, w_a2, b_a2, g_gla, w_out,
                  g_norm2, w_up, w_conv, b_conv, w_down):
    o, cols = 0, []
    for n in (MLA_Q_RANK, MLA_KV_RANK, MLA_ROPE, GLA_K_ALL, GLA_K_ALL, GLA_V_ALL, GLA_GATE_RANK, GLA_V_ALL):
        cols.append(w_in[:, o:o + n])
        o += n
    q_lat, kv_lat, kpe, gq, gk, gv, g_lr, og = cols
    misc_pad = jnp.zeros((D_MODEL, LANES - MLA_ROPE - GLA_GATE_RANK), w_in.dtype)
    w_in_r = jnp.concatenate([q_lat, kv_lat, gq, gk, gv, og, kpe, g_lr, misc_pad], axis=1).astype(BF16)

    wq = w_uq.reshape(MLA_Q_RANK, MLA_HEADS, MLA_QK)
    n_, r1, r2 = wq[..., :MLA_NOPE], wq[..., MLA_NOPE:MLA_NOPE + HALF_ROPE], wq[..., MLA_NOPE + HALF_ROPE:]
    w_uq_p = jnp.concatenate([n_, r1, r2, r2, r1], axis=-1).reshape(MLA_Q_RANK, MLA_HEADS * HEAD_PAD).astype(BF16)

    sc = MLA_QK ** -0.5 * math.log2(math.e)
    gn, g1, g2 = g_qn[:MLA_NOPE], g_qn[MLA_NOPE:MLA_NOPE + HALF_ROPE], g_qn[MLA_NOPE + HALF_ROPE:]
    z16, z64 = jnp.zeros((HALF_ROPE,), F32), jnp.zeros((MLA_NOPE,), F32)
    gq3 = jnp.stack([jnp.concatenate([gn, z16, z16, z16, z16]),
                     jnp.concatenate([z64, g1, g2, z16, z16]),
                     jnp.concatenate([z64, z16, z16, g2, g1])]) * sc
    kn, k1, k2 = g_kn[:MLA_NOPE], g_kn[MLA_NOPE:MLA_NOPE + HALF_ROPE], g_kn[MLA_NOPE + HALF_ROPE:]
    gk3 = jnp.stack([jnp.concatenate([kn, z16, z16, z16, z16]),
                     jnp.concatenate([z64, k1, k2, -k1, k2]),
                     jnp.concatenate([z64, -k2, k1, k2, k1])])
    eye = jnp.eye(HALF_ROPE, dtype=F32)
    zz = jnp.zeros((HALF_ROPE, HALF_ROPE), F32)
    pick1 = jnp.concatenate([eye, zz], axis=0)
    pick2 = jnp.concatenate([zz, eye], axis=0)
    z_n = jnp.zeros((MLA_ROPE, MLA_NOPE), F32)
    sel = jnp.concatenate([z_n, pick1, pick2, pick1, pick2, z_n, pick2, pick1, pick2, pick1], axis=1).astype(BF16)

    w_a2_p = jnp.zeros((LANES, GLA_K_ALL), F32).at[MLA_ROPE:MLA_ROPE + GLA_GATE_RANK].set(w_a2).astype(BF16)

    wo = w_out[:MLA_HEADS * MLA_V].reshape(MLA_HEADS, MLA_V, D_MODEL)
    w_out_mla = jnp.concatenate([jnp.zeros_like(wo), wo], axis=1).reshape(MLA_HEADS * HEAD_PAD, D_MODEL).astype(BF16)
    return {
        "score_bound": 1.01 * sc * MLA_QK * jnp.max(jnp.abs(g_qn)) * jnp.max(jnp.abs(g_kn)),
        "g_norm1": g_norm1.reshape(1, -1), "w_in": w_in_r, "g_qa": g_qa.reshape(1, -1), "w_uq": w_uq_p,
        "gq3": gq3, "g_kva": g_kva.reshape(1, -1), "w_ukv": w_ukv.astype(BF16), "sel": sel, "gk3": gk3,
        "w_a2": w_a2_p, "b_a2": b_a2.reshape(1, -1), "g_gla": g_gla.reshape(1, -1),
        "w_out_mla": w_out_mla, "w_out_gla": w_out[MLA_HEADS * MLA_V:].astype(BF16),
        "g_norm2": g_norm2.reshape(1, -1), "w_up": w_up.astype(BF16), "w_conv": w_conv,
        "b_conv": b_conv.reshape(1, -1), "w_down": w_down.astype(BF16),
    }


def _rope_tables(n):
    inv = 1.0 / (ROPE_THETA ** (jnp.arange(HALF_ROPE, dtype=F32) / HALF_ROPE))
    ang = jnp.arange(n, dtype=F32)[:, None] * jnp.tile(inv, LANES // HALF_ROPE)[None, :]
    return jnp.cos(ang), jnp.sin(ang)


def _state_to_t(s):
    b = s.shape[0]
    return s.reshape(b, GLA_K_ALL, GLA_DV).transpose(0, 2, 1)


def _state_from_t(st):
    b = st.shape[0]
    return st.transpose(0, 2, 1).reshape(b, GLA_HEADS, GLA_DK, GLA_DV)


def _pick_tile(n, pref):
    t = min(n, pref)
    while n % t:
        t //= 2
    return t


def _layer(x, mods, w, cos_t, sin_t, cache_ckv, cache_kpe, state, hist):
    b, l, _ = x.shape
    tm = _pick_tile(l, 512)
    pos0 = 0 if cache_ckv is None else cache_ckv.shape[1]
    ckv, kpe, q, gq, gk, gv, lg, og = _inproj(x, mods, w, cos_t, sin_t, tm, pos0)
    if cache_ckv is None:
        k, vt = _keys(ckv, kpe, w, cos_t, sin_t, tm, True)
        o_mla = _attn_prompt(q, k, vt, tm, w["score_bound"])
    else:
        n_valid = pos0 + l
        lk = -(-n_valid // LANES) * LANES
        ckv_all = jnp.concatenate([cache_ckv, ckv, jnp.zeros((b, lk - n_valid, MLA_KV_RANK), F32)], axis=1)
        kpe_all = jnp.concatenate([cache_kpe, kpe, jnp.zeros((b, lk - n_valid, MLA_ROPE), F32)], axis=1)
        k, v = _keys(ckv_all, kpe_all, w, cos_t, sin_t, lk, False)
        o_mla = _attn_sample(q, k, v, n_valid)
    o_gla, st = _gla(gq, gk, gv, lg, og, _state_to_t(state), w, tm)
    y, new_hist = _ffn(x, o_mla, o_gla, mods, hist, w, tm)
    return y, ckv, kpe, _state_from_t(st), new_hist


def kernel(x_prompt, x_sample, c_prompt, c_sample, cache_ckv, cache_kpe, state_gla, state_ffn_conv, w_ada, b_ada, g_norm1, w_in, g_qa, w_uq, g_qn, g_kva, w_ukv, g_kn, w_a2, b_a2, g_gla, w_out, g_norm2, w_up, w_conv, b_conv, w_down):
    depth = w_ada.shape[0]
    bp, lp, _ = x_prompt.shape
    bs, ls, _ = x_sample.shape
    n_pos = max(lp, cache_ckv.shape[2] + ls)
    n_pos = -(-n_pos // LANES) * LANES
    cos_t, sin_t = _rope_tables(n_pos)
    yp, ys = x_prompt, x_sample
    outs = [[] for _ in range(8)]
    for i in range(depth):
        w = _prep_weights(w_in[i], g_norm1[i], g_qa[i], w_uq[i], g_qn[i], g_kva[i], w_ukv[i], g_kn[i],
                          w_a2[i], b_a2[i], g_gla[i], w_out[i], g_norm2[i], w_up[i], w_conv[i], b_conv[i],
                          w_down[i])
        mods = _adaln(jnp.concatenate([c_prompt, c_sample], axis=0), w_ada[i], b_ada[i])
        mods = mods.reshape(bp + bs, 6, D_MODEL)
        zero_state = jnp.zeros((bp, GLA_HEADS, GLA_DK, GLA_DV), F32)
        zero_hist = jnp.zeros((bp, CONV_W - 1, FFN_DIM), F32)
        yp, a, b_, s, h = _layer(yp, mods[:bp], w, cos_t, sin_t, None, None, zero_state, zero_hist)
        for lst, val in zip(outs[:4], (a, b_, s, h)):
            lst.append(val)
        ys, a, b_, s, h = _layer(ys, mods[bp:], w, cos_t, sin_t, cache_ckv[i], cache_kpe[i],
                                 state_gla[i], state_ffn_conv[i])
        for lst, val in zip(outs[4:], (a, b_, s, h)):
            lst.append(val)
    return (yp, ys) + tuple(jnp.stack(o) for o in outs)
```

```python
import functools
import math

import jax
import jax.numpy as jnp
from jax import lax
from jax.experimental import pallas as pl
from jax.experimental.pallas import tpu as pltpu

F32 = jnp.float32
BF16 = jnp.bfloat16

D_MODEL = 1024
CHUNK = 64
CHUNK_SHIFT = 6
MLA_HEADS = 8
MLA_NOPE = 64
MLA_ROPE = 32
MLA_QK = MLA_NOPE + MLA_ROPE
MLA_V = 64
MLA_Q_RANK = 384
MLA_KV_RANK = 256
ROPE_THETA = 10000.0
GLA_HEADS = 4
GLA_DK = 64
GLA_DV = 128
GLA_GATE_RANK = 16
GLA_TAU = 16.0
FFN_DIM = 2816
CONV_W = 3
EPS = 1e-6

LANES = 128
HEAD_PAD = LANES
HALF_ROPE = MLA_ROPE // 2
GLA_K_ALL = GLA_HEADS * GLA_DK
GLA_V_ALL = GLA_HEADS * GLA_DV
C_QLAT = 0
C_KV = C_QLAT + MLA_Q_RANK
C_GQ = C_KV + MLA_KV_RANK
C_GK = C_GQ + GLA_K_ALL
C_GV = C_GK + GLA_K_ALL
C_OG = C_GV + GLA_V_ALL
C_MISC = C_OG + GLA_V_ALL
IN_COLS_PAD = C_MISC + LANES
FFN_TILE = 256
EXP_CLAMP = 80.0
VMEM_LIMIT = 56 * 1024 * 1024


def _const_spec(shape):
    nd = len(shape)
    return pl.BlockSpec(shape, lambda *_: (0,) * nd, pipeline_mode=pl.Buffered(1))


def _params(*sem):
    return pltpu.CompilerParams(dimension_semantics=sem, vmem_limit_bytes=VMEM_LIMIT)


def _rms_rows(x):
    return x * lax.rsqrt(jnp.mean(x * x, axis=-1, keepdims=True) + EPS)


def _ada_kernel(c_ref, w_ref, b_ref, o_ref):
    c = c_ref[...]
    s = c * jax.nn.sigmoid(c)
    o_ref[...] = jnp.dot(s, w_ref[...], precision=lax.Precision.HIGHEST,
                         preferred_element_type=F32) + b_ref[...]


def _adaln(c, w_ada, b_ada):
    n, d = c.shape
    cols = w_ada.shape[1]
    tn = 1024
    return pl.pallas_call(
        _ada_kernel,
        grid=(cols // tn,),
        in_specs=[pl.BlockSpec((n, d), lambda j: (0, 0)),
                  pl.BlockSpec((d, tn), lambda j: (0, j)),
                  pl.BlockSpec((1, tn), lambda j: (0, j))],
        out_specs=pl.BlockSpec((n, tn), lambda j: (0, j)),
        out_shape=jax.ShapeDtypeStruct((n, cols), F32),
        compiler_params=_params("arbitrary"),
        name="adaln",
    )(c, w_ada, b_ada.reshape(1, cols))


def _inproj_kernel(x_ref, mods_ref, g1_ref, win_ref, gqa_ref, wuq_ref, gq3_ref, cos_ref, sin_ref,
                   gkva_ref, wa2_ref, ba2_ref,
                   ckv_ref, kpe_ref, q_ref, gq_ref, gk_ref, gv_ref, lg_ref, og_ref):
    x = x_ref[0]
    shift = mods_ref[0, 0:1, :]
    scale = mods_ref[0, 1:2, :]
    h = _rms_rows(x) * g1_ref[...] * (1.0 + scale) + shift
    p = jnp.dot(h.astype(BF16), win_ref[...], preferred_element_type=F32)

    ckv_ref[0] = _rms_rows(p[:, C_KV:C_GQ]) * gkva_ref[...]
    misc = p[:, C_MISC:IN_COLS_PAD]
    kpe_ref[0] = misc[:, 0:MLA_ROPE]

    qa = _rms_rows(p[:, C_QLAT:C_KV]) * gqa_ref[...]
    qu = jnp.dot(qa.astype(BF16), wuq_ref[...], preferred_element_type=F32)
    tq = gq3_ref[0:1, :] + cos_ref[...] * gq3_ref[1:2, :] + sin_ref[...] * gq3_ref[2:3, :]
    lane = lax.broadcasted_iota(jnp.int32, (1, HEAD_PAD), 1)
    real = (lane < MLA_QK).astype(F32)
    for hd in range(MLA_HEADS):
        xh = qu[:, hd * HEAD_PAD:(hd + 1) * HEAD_PAD]
        ss = jnp.sum(xh * xh * real, axis=-1, keepdims=True)
        r = lax.rsqrt(ss * (1.0 / MLA_QK) + EPS)
        q_ref[0, :, hd * HEAD_PAD:(hd + 1) * HEAD_PAD] = (xh * r * tq).astype(BF16)

    gq_ref[0] = (p[:, C_GQ:C_GK] * (GLA_DK ** -0.5)).astype(BF16)
    gk_ref[0] = p[:, C_GK:C_GV].astype(BF16)
    gv_ref[0] = p[:, C_GV:C_OG].astype(BF16)
    og_ref[0] = p[:, C_OG:C_MISC].astype(BF16)
    z = jnp.dot(misc.astype(BF16), wa2_ref[...], preferred_element_type=F32) + ba2_ref[...]
    log_sig = jnp.minimum(z, 0.0) - jnp.log1p(jnp.exp(-jnp.abs(z)))
    lg_ref[0] = log_sig * (1.0 / GLA_TAU)


def _inproj(x, mods, w, cos_t, sin_t, tm, pos0):
    b, l, d = x.shape
    nt = l // tm
    p0 = pos0 // tm
    row = lambda bi, li: (bi, li, 0)
    tab = lambda bi, li: (li + p0, 0)
    outs = [(MLA_KV_RANK, F32), (MLA_ROPE, F32), (MLA_HEADS * HEAD_PAD, BF16), (GLA_K_ALL, BF16),
            (GLA_K_ALL, BF16), (GLA_V_ALL, BF16), (GLA_K_ALL, F32), (GLA_V_ALL, BF16)]
    return pl.pallas_call(
        _inproj_kernel,
        grid=(b, nt),
        in_specs=[pl.BlockSpec((1, tm, d), row),
                  pl.BlockSpec((1, 6, d), lambda bi, li: (bi, 0, 0)),
                  _const_spec((1, d)),
                  _const_spec((d, IN_COLS_PAD)),
                  _const_spec((1, MLA_Q_RANK)),
                  _const_spec((MLA_Q_RANK, MLA_HEADS * HEAD_PAD)),
                  _const_spec((3, HEAD_PAD)),
                  pl.BlockSpec((tm, HEAD_PAD), tab),
                  pl.BlockSpec((tm, HEAD_PAD), tab),
                  _const_spec((1, MLA_KV_RANK)),
                  _const_spec((LANES, GLA_K_ALL)),
                  _const_spec((1, GLA_K_ALL))],
        out_specs=[pl.BlockSpec((1, tm, n), row) for n, _ in outs],
        out_shape=[jax.ShapeDtypeStruct((b, l, n), dt) for n, dt in outs],
        compiler_params=_params("arbitrary", "arbitrary"),
        name="inproj",
    )(x, mods, w["g_norm1"], w["w_in"], w["g_qa"], w["w_uq"], w["gq3"], cos_t, sin_t,
      w["g_kva"], w["w_a2"], w["b_a2"])


def _keys_kernel(ckv_ref, kpe_ref, wukv_ref, sel_ref, gk3_ref, cos_ref, sin_ref, k_ref, v_ref, *,
                 transpose_v):
    kv = jnp.dot(ckv_ref[0].astype(BF16), wukv_ref[...], preferred_element_type=F32)
    kpe = kpe_ref[0]
    kpe_hi = kpe.astype(BF16)
    kpe_lo = (kpe - kpe_hi.astype(F32)).astype(BF16)
    uv = (jnp.dot(kpe_hi, sel_ref[...], preferred_element_type=F32)
          + jnp.dot(kpe_lo, sel_ref[...], preferred_element_type=F32))
    rot = (uv[:, :HEAD_PAD] * (cos_ref[...] * gk3_ref[1:2, :])
           + uv[:, HEAD_PAD:] * (sin_ref[...] * gk3_ref[2:3, :]))
    sp = jnp.sum(kpe * kpe, axis=-1, keepdims=True)
    lane = lax.broadcasted_iota(jnp.int32, (1, HEAD_PAD), 1)
    is_nope = lane < MLA_NOPE
    g_nope = gk3_ref[0:1, :]
    for hd in range(MLA_HEADS):
        blk = kv[:, hd * HEAD_PAD:(hd + 1) * HEAD_PAD]
        kn = jnp.where(is_nope, blk, 0.0)
        ss = jnp.sum(kn * kn, axis=-1, keepdims=True) + sp
        r = lax.rsqrt(ss * (1.0 / MLA_QK) + EPS)
        k_ref[0, :, hd * HEAD_PAD:(hd + 1) * HEAD_PAD] = ((kn * g_nope + rot) * r).astype(BF16)
        vp = jnp.where(is_nope, 1.0, blk)
        if transpose_v:
            v_ref[0, hd * HEAD_PAD:(hd + 1) * HEAD_PAD, :] = vp.T.astype(BF16)
        else:
            v_ref[0, :, hd * HEAD_PAD:(hd + 1) * HEAD_PAD] = vp.astype(BF16)


def _keys(ckv, kpe, w, cos_t, sin_t, tm, transpose_v):
    b, l, _ = ckv.shape
    row = lambda bi, li: (bi, li, 0)
    tab = lambda bi, li: (li, 0)
    hw = MLA_HEADS * HEAD_PAD
    if transpose_v:
        v_spec = pl.BlockSpec((1, hw, tm), lambda bi, li: (bi, 0, li))
        v_shape = jax.ShapeDtypeStruct((b, hw, l), BF16)
    else:
        v_spec = pl.BlockSpec((1, tm, hw), row)
        v_shape = jax.ShapeDtypeStruct((b, l, hw), BF16)
    return pl.pallas_call(
        functools.partial(_keys_kernel, transpose_v=transpose_v),
        grid=(b, l // tm),
        in_specs=[pl.BlockSpec((1, tm, MLA_KV_RANK), row),
                  pl.BlockSpec((1, tm, MLA_ROPE), row),
                  _const_spec((MLA_KV_RANK, hw)),
                  _const_spec((MLA_ROPE, 2 * HEAD_PAD)),
                  _const_spec((3, HEAD_PAD)),
                  pl.BlockSpec((tm, HEAD_PAD), tab),
                  pl.BlockSpec((tm, HEAD_PAD), tab)],
        out_specs=[pl.BlockSpec((1, tm, hw), row), v_spec],
        out_shape=[jax.ShapeDtypeStruct((b, l, hw), BF16), v_shape],
        compiler_params=_params("arbitrary", "arbitrary"),
        name="keys",
    )(ckv, kpe, w["w_ukv"], w["sel"], w["gk3"], cos_t, sin_t)


_NT = (((1,), (1,)), ((), ()))
_TN = (((0,), (0,)), ((), ()))
MASKED = -1e30
EXP2_SAFE = 100.0


def _finalize_tile(acc_scr, o_ref, i, tq):
    acc = acc_scr[...]
    o_ref[0, pl.ds(pl.multiple_of(i * tq, tq), tq), :] = (acc / acc[0:1, :]).T.astype(BF16)


def _chunk_visible(tq):
    kc = lax.broadcasted_iota(jnp.int32, (tq, tq), 0) >> CHUNK_SHIFT
    qc = lax.broadcasted_iota(jnp.int32, (tq, tq), 1) >> CHUNK_SHIFT
    return kc <= qc


ATTN_LAG = 2
ATTN_UNROLL = 6


def _attn_bounded_kernel(iq_tab, ia_tab, j_tab, d_tab, q_ref, k_ref, vt_ref, mask_ref, o_ref, p_buf, acc_scr,
                         *, tq, nq, n_iter):
    p_buf[...] = jnp.zeros_like(p_buf)
    acc_scr[...] = jnp.zeros_like(acc_scr)
    n_slots = ATTN_LAG + 1

    def sub_step(t, slot):
        tp = jnp.maximum(t - ATTN_LAG, 0)
        jp = j_tab[tp]
        vt = vt_ref[0, :, pl.ds(pl.multiple_of(jp * tq, tq), tq)]
        cols = pl.ds(pl.multiple_of(ia_tab[tp] * tq, tq), tq)
        acc_scr[:, cols] = (jnp.where(jp == 0, 0.0, acc_scr[:, cols])
                            + jnp.dot(vt, p_buf[(slot + 1) % n_slots], preferred_element_type=F32))
        kb = k_ref[0, pl.ds(pl.multiple_of(j_tab[t] * tq, tq), tq), :]
        qt = q_ref[0, pl.ds(pl.multiple_of(iq_tab[t] * tq, tq), tq), :]
        p = jnp.exp2(lax.dot_general(kb, qt, _NT, preferred_element_type=F32)).astype(BF16)
        p_buf[slot] = p * mask_ref[d_tab[t]]

    def body(it, carry):
        for u in range(ATTN_UNROLL):
            sub_step(ATTN_UNROLL * it + u, u % n_slots)
        return carry

    lax.fori_loop(0, n_iter, body, 0)
    for i in range(nq):
        acc = acc_scr[:, i * tq:(i + 1) * tq]
        o_ref[0, i * tq:(i + 1) * tq, :] = (acc / acc[0:1, :]).T.astype(BF16)


def _attn_online_kernel(i_tab, j_tab, q_ref, k_ref, vt_ref, o_ref, s_buf, p_buf, a_buf, m_scr, acc_scr,
                        *, tq, n_steps, n_iter):
    s_buf[...] = jnp.zeros_like(s_buf)
    p_buf[...] = jnp.zeros_like(p_buf)
    a_buf[...] = jnp.zeros_like(a_buf)
    m_scr[...] = jnp.zeros_like(m_scr)
    acc_scr[...] = jnp.zeros_like(acc_scr)

    def sub_step(t, slot):
        other = 1 - slot
        t2 = jnp.maximum(t - 2, 0)
        i2, j2 = i_tab[t2], j_tab[t2]
        vt = vt_ref[0, :, pl.ds(pl.multiple_of(j2 * tq, tq), tq)]
        acc_scr[...] = a_buf[slot] * acc_scr[...] + jnp.dot(vt, p_buf[slot], preferred_element_type=F32)
        t1 = jnp.maximum(t - 1, 0)
        s = s_buf[other]
        m_old = jnp.where(j_tab[t1] == 0, -jnp.inf, m_scr[...])
        m_new = jnp.maximum(m_old, jnp.max(s, axis=0, keepdims=True))
        a_buf[other] = jnp.exp2(m_old - m_new)
        p_buf[other] = jnp.exp2(s - m_new).astype(BF16)
        m_scr[...] = m_new
        t0 = jnp.minimum(t, n_steps - 1)
        i0, j0 = i_tab[t0], j_tab[t0]
        kb = k_ref[0, pl.ds(pl.multiple_of(j0 * tq, tq), tq), :]
        qt = q_ref[0, pl.ds(pl.multiple_of(i0 * tq, tq), tq), :]
        s_buf[slot] = lax.dot_general(kb, qt, _NT, preferred_element_type=F32)

        @pl.when(i0 == j0)
        def _():
            s_buf[slot] = jnp.where(_chunk_visible(tq), s_buf[slot], MASKED)

        @pl.when(jnp.logical_and(jnp.logical_and(t >= 2, t - 2 < n_steps), i2 == j2))
        def _():
            _finalize_tile(acc_scr, o_ref, i2, tq)

    def body(it, carry):
        sub_step(2 * it, 0)
        sub_step(2 * it + 1, 1)
        return carry

    lax.fori_loop(0, n_iter, body, 0)


def _attn_specs(b, l, n_tabs, extra_in=()):
    qmap = lambda bi, hi, *_: (bi, 0, hi)
    return dict(
        num_scalar_prefetch=n_tabs,
        grid=(b, MLA_HEADS),
        in_specs=[pl.BlockSpec((1, l, HEAD_PAD), qmap),
                  pl.BlockSpec((1, l, HEAD_PAD), qmap),
                  pl.BlockSpec((1, HEAD_PAD, l), lambda bi, hi, *_: (bi, hi, 0)), *extra_in],
        out_specs=pl.BlockSpec((1, l, HEAD_PAD), qmap))


def _attn_online_call(q, k, vt, tq):
    b, l, hw = q.shape
    nq = l // tq
    pairs = [(i, j) for i in range(nq) for j in range(i + 1)]
    n_steps = len(pairs)
    n_iter = (n_steps + 3) // 2
    pairs = pairs + [pairs[-1]] * (2 * n_iter - n_steps)
    i_tab = jnp.asarray([p_[0] for p_ in pairs], jnp.int32)
    j_tab = jnp.asarray([p_[1] for p_ in pairs], jnp.int32)
    grid_spec = pltpu.PrefetchScalarGridSpec(
        **_attn_specs(b, l, 2),
        scratch_shapes=[pltpu.VMEM((2, tq, tq), F32), pltpu.VMEM((2, tq, tq), BF16),
                        pltpu.VMEM((2, 1, tq), F32), pltpu.VMEM((1, tq), F32),
                        pltpu.VMEM((HEAD_PAD, tq), F32)])
    return pl.pallas_call(
        functools.partial(_attn_online_kernel, tq=tq, n_steps=n_steps, n_iter=n_iter),
        grid_spec=grid_spec,
        out_shape=jax.ShapeDtypeStruct((b, l, hw), BF16),
        compiler_params=_params("arbitrary", "arbitrary"),
        name="attn_online",
    )(i_tab, j_tab, q, k, vt)


def _attn_bounded_call(q, k, vt, tq):
    b, l, hw = q.shape
    nq = l // tq
    pairs = [(i, j) for i in range(nq) for j in range(i + 1)]
    n_iter = -(-(len(pairs) + ATTN_LAG) // ATTN_UNROLL)
    n_dummy = n_iter * ATTN_UNROLL - len(pairs)
    iq_tab = jnp.asarray([p_[0] for p_ in pairs] + [nq - 1] * n_dummy, jnp.int32)
    ia_tab = jnp.asarray([p_[0] for p_ in pairs] + [nq] * n_dummy, jnp.int32)
    j_tab = jnp.asarray([p_[1] for p_ in pairs] + [0] * n_dummy, jnp.int32)
    d_tab = jnp.asarray([int(p_[0] == p_[1]) for p_ in pairs] + [0] * n_dummy, jnp.int32)
    masks = jnp.stack([jnp.ones((tq, tq), BF16), _chunk_visible(tq).astype(BF16)])
    grid_spec = pltpu.PrefetchScalarGridSpec(
        **_attn_specs(b, l, 4, [pl.BlockSpec((2, tq, tq), lambda *_: (0, 0, 0), pipeline_mode=pl.Buffered(1))]),
        scratch_shapes=[pltpu.VMEM((ATTN_LAG + 1, tq, tq), BF16), pltpu.VMEM((HEAD_PAD, (nq + 1) * tq), F32)])
    return pl.pallas_call(
        functools.partial(_attn_bounded_kernel, tq=tq, nq=nq, n_iter=n_iter),
        grid_spec=grid_spec,
        out_shape=jax.ShapeDtypeStruct((b, l, hw), BF16),
        compiler_params=_params("arbitrary", "arbitrary"),
        name="attn_bounded",
    )(iq_tab, ia_tab, j_tab, d_tab, q, k, vt, masks)


def _attn_prompt(q, k, vt, tq, score_bound):
    return lax.cond(score_bound <= EXP2_SAFE,
                    functools.partial(_attn_bounded_call, tq=tq),
                    functools.partial(_attn_online_call, tq=tq),
                    q, k, vt)


def _attn_sample_kernel(q_ref, k_ref, v_ref, o_ref, *, n_valid):
    s = lax.dot_general(q_ref[0], k_ref[0], _NT, preferred_element_type=F32)
    col = lax.broadcasted_iota(jnp.int32, s.shape, 1)
    s = jnp.where(col < n_valid, s, MASKED)
    p = jnp.exp2(s - jnp.max(s, axis=-1, keepdims=True))
    acc = jnp.dot(p.astype(BF16), v_ref[0], preferred_element_type=F32)
    o_ref[0] = (acc / acc[:, 0:1]).astype(BF16)


def _attn_sample(q, k, v, n_valid):
    b, t, hw = q.shape
    lk = k.shape[1]
    qmap = lambda bi, hi: (bi, 0, hi)
    return pl.pallas_call(
        functools.partial(_attn_sample_kernel, n_valid=n_valid),
        grid=(b, MLA_HEADS),
        in_specs=[pl.BlockSpec((1, t, HEAD_PAD), qmap),
                  pl.BlockSpec((1, lk, HEAD_PAD), qmap),
                  pl.BlockSpec((1, lk, HEAD_PAD), qmap)],
        out_specs=pl.BlockSpec((1, t, HEAD_PAD), qmap),
        out_shape=jax.ShapeDtypeStruct((b, t, hw), BF16),
        compiler_params=_params("arbitrary", "arbitrary"),
        name="attn_sample",
    )(q, k, v)


def _gla_kernel(gq_ref, gk_ref, gv_ref, lg_ref, og_ref, tri_ref, ggla_ref, st0_ref,
                o_ref, stf_ref, st_scr, b_scr, *, tg):
    li = pl.program_id(1)

    @pl.when(li == 0)
    def _():
        st_scr[...] = st0_ref[0]

    lg = lg_ref[0]
    lg_hi = lg.astype(BF16)
    lg_lo = (lg - lg_hi.astype(F32)).astype(BF16)
    tri = tri_ref[...]
    b_scr[...] = (jnp.dot(tri, lg_hi, preferred_element_type=F32)
                  + jnp.dot(tri, lg_lo, preferred_element_type=F32))

    lane = lax.broadcasted_iota(jnp.int32, (1, GLA_K_ALL), 1)
    head_of_lane = lane >> CHUNK_SHIFT
    ri = lax.broadcasted_iota(jnp.int32, (GLA_HEADS * CHUNK, CHUNK), 0) & (CHUNK - 1)
    cj = lax.broadcasted_iota(jnp.int32, (GLA_HEADS * CHUNK, CHUNK), 1)
    causal = cj <= ri
    g_out = ggla_ref[...]

    for c in range(tg // CHUNK):
        r0 = c * CHUNK
        b = b_scr[r0:r0 + CHUNK, :]
        b_mid = b[CHUNK // 2 - 1:CHUNK // 2, :]
        b_last = b[CHUNK - 1:CHUNK, :]
        q = gq_ref[0, r0:r0 + CHUNK, :].astype(F32)
        k = gk_ref[0, r0:r0 + CHUNK, :].astype(F32)
        qe = q * jnp.exp(jnp.minimum(b - b_mid, EXP_CLAMP))
        ke = (k * jnp.exp(jnp.minimum(b_mid - b, EXP_CLAMP))).astype(BF16)
        qb = q * jnp.exp(b)
        kd = k * jnp.exp(b_last - b)
        zero = jnp.zeros_like(q)
        qe_st = jnp.concatenate([jnp.where(head_of_lane == hd, qe, zero) for hd in range(GLA_HEADS)],
                                axis=0).astype(BF16)
        qb_st = jnp.concatenate([jnp.where(head_of_lane == hd, qb, zero) for hd in range(GLA_HEADS)],
                                axis=0).astype(BF16)
        a_st = lax.dot_general(qe_st, ke, _NT, preferred_element_type=F32)
        a_st = jnp.where(causal, a_st, 0.0).astype(BF16)
        st = st_scr[...]
        o_inter = lax.dot_general(qb_st, st.astype(BF16), _NT, preferred_element_type=F32)
        upd = jnp.zeros_like(st)
        for hd in range(GLA_HEADS):
            vh = gv_ref[0, r0:r0 + CHUNK, hd * GLA_DV:(hd + 1) * GLA_DV]
            o = (o_inter[hd * CHUNK:(hd + 1) * CHUNK, :]
                 + jnp.dot(a_st[hd * CHUNK:(hd + 1) * CHUNK, :], vh, preferred_element_type=F32))
            og = og_ref[0, r0:r0 + CHUNK, hd * GLA_DV:(hd + 1) * GLA_DV].astype(F32)
            o = _rms_rows(o) * g_out * (og * jax.nn.sigmoid(og))
            o_ref[0, r0:r0 + CHUNK, hd * GLA_DV:(hd + 1) * GLA_DV] = o.astype(BF16)
            kd_h = jnp.where(head_of_lane == hd, kd, zero).astype(BF16)
            upd = upd + lax.dot_general(vh, kd_h, _TN, preferred_element_type=F32)
        st_scr[...] = st * jnp.exp(b_last) + upd

    stf_ref[0] = st_scr[...]


def _gla(gq, gk, gv, lg, og, st0, w, tg):
    b, l, _ = gq.shape
    row = lambda bi, li: (bi, li, 0)
    fix = lambda bi, li: (bi, 0, 0)
    blk = jnp.arange(tg) // CHUNK
    tri = ((blk[:, None] == blk[None, :]) & (jnp.arange(tg)[None, :] <= jnp.arange(tg)[:, None])).astype(BF16)
    return pl.pallas_call(
        functools.partial(_gla_kernel, tg=tg),
        grid=(b, l // tg),
        in_specs=[pl.BlockSpec((1, tg, GLA_K_ALL), row),
                  pl.BlockSpec((1, tg, GLA_K_ALL), row),
                  pl.BlockSpec((1, tg, GLA_V_ALL), row),
                  pl.BlockSpec((1, tg, GLA_K_ALL), row),
                  pl.BlockSpec((1, tg, GLA_V_ALL), row),
                  _const_spec((tg, tg)),
                  _const_spec((1, GLA_DV)),
                  pl.BlockSpec((1, GLA_DV, GLA_K_ALL), fix)],
        out_specs=[pl.BlockSpec((1, tg, GLA_V_ALL), row),
                   pl.BlockSpec((1, GLA_DV, GLA_K_ALL), fix)],
        out_shape=[jax.ShapeDtypeStruct((b, l, GLA_V_ALL), BF16),
                   jax.ShapeDtypeStruct((b, GLA_DV, GLA_K_ALL), F32)],
        scratch_shapes=[pltpu.VMEM((GLA_DV, GLA_K_ALL), F32), pltpu.VMEM((tg, GLA_K_ALL), F32)],
        compiler_params=_params("arbitrary", "arbitrary"),
        name="gla",
    )(gq, gk, gv, lg, og, tri, w["g_gla"], st0)


def _ffn_kernel(x_ref, om_ref, ogl_ref, mods_ref, g2_ref, wom_ref, wog_ref, wup_ref, wcv_ref, bcv_ref,
                wdn_ref, hist_ref, y_ref, nh_ref, carry_scr, a_scr, act_scr, *, tm):
    li = pl.program_id(1)
    hist_rows = CONV_W - 1
    pad = 8

    @pl.when(li == 0)
    def _():
        carry_scr[...] = jnp.zeros_like(carry_scr)
        carry_scr[pad - hist_rows:pad, :] = hist_ref[0]

    mixed = (jnp.dot(om_ref[0], wom_ref[...], preferred_element_type=F32)
             + jnp.dot(ogl_ref[0], wog_ref[...], preferred_element_type=F32))
    x1 = x_ref[0] + mods_ref[0, 2:3, :] * mixed
    h = (_rms_rows(x1) * g2_ref[...] * (1.0 + mods_ref[0, 4:5, :]) + mods_ref[0, 3:4, :]).astype(BF16)

    for f in range(FFN_DIM // FFN_TILE):
        c0 = f * FFN_TILE
        a = jnp.dot(h, wup_ref[:, c0:c0 + FFN_TILE], preferred_element_type=F32)
        g = jnp.dot(h, wup_ref[:, FFN_DIM + c0:FFN_DIM + c0 + FFN_TILE], preferred_element_type=F32)
        a_scr[0:pad, :] = carry_scr[:, c0:c0 + FFN_TILE]
        a_scr[pad:pad + tm, :] = a
        carry_scr[:, c0:c0 + FFN_TILE] = a[tm - pad:tm, :]
        nh_ref[0, :, c0:c0 + FFN_TILE] = a[tm - hist_rows:tm, :]
        conv = (bcv_ref[:, c0:c0 + FFN_TILE]
                + wcv_ref[2:3, c0:c0 + FFN_TILE] * a
                + wcv_ref[1:2, c0:c0 + FFN_TILE] * a_scr[pad - 1:pad - 1 + tm, :]
                + wcv_ref[0:1, c0:c0 + FFN_TILE] * a_scr[pad - 2:pad - 2 + tm, :])
        act = (jax.nn.gelu(conv) * g).astype(BF16)
        act_scr[:, c0:c0 + FFN_TILE] = act

    y = jnp.dot(act_scr[...], wdn_ref[...], preferred_element_type=F32)
    y_ref[0] = x1 + mods_ref[0, 5:6, :] * y


def _ffn(x, o_mla, o_gla, mods, hist, w, tm):
    b, l, d = x.shape
    row = lambda bi, li: (bi, li, 0)
    fix = lambda bi, li: (bi, 0, 0)
    hw = MLA_HEADS * HEAD_PAD
    return pl.pallas_call(
        functools.partial(_ffn_kernel, tm=tm),
        grid=(b, l // tm),
        in_specs=[pl.BlockSpec((1, tm, d), row),
                  pl.BlockSpec((1, tm, hw), row),
                  pl.BlockSpec((1, tm, GLA_V_ALL), row),
                  pl.BlockSpec((1, 6, d), fix),
                  _const_spec((1, d)),
                  _const_spec((hw, d)),
                  _const_spec((GLA_V_ALL, d)),
                  _const_spec((d, 2 * FFN_DIM)),
                  _const_spec((CONV_W, FFN_DIM)),
                  _const_spec((1, FFN_DIM)),
                  _const_spec((FFN_DIM, d)),
                  pl.BlockSpec((1, CONV_W - 1, FFN_DIM), fix)],
        out_specs=[pl.BlockSpec((1, tm, d), row),
                   pl.BlockSpec((1, CONV_W - 1, FFN_DIM), fix)],
        out_shape=[jax.ShapeDtypeStruct((b, l, d), F32),
                   jax.ShapeDtypeStruct((b, CONV_W - 1, FFN_DIM), F32)],
        scratch_shapes=[pltpu.VMEM((8, FFN_DIM), F32),
                        pltpu.VMEM((tm + 8, FFN_TILE), F32),
                        pltpu.VMEM((tm, FFN_DIM), BF16)],
        compiler_params=_params("arbitrary", "arbitrary"),
        name="ffn",
    )(x, o_mla, o_gla, mods, w["g_norm2"], w["w_out_mla"], w["w_out_gla"], w["w_up"], w["w_conv"],
      w["b_conv"], w["w_down"], hist)


def _prep_weights(w_in, g_norm1, g_qa, w_uq, g_qn, g_kva, w_ukv, g_kn, w_a2, b_a2, g_gla, w_out,
                  g_norm2, w_up, w_conv, b_conv, w_down):
    o, cols = 0, []
    for n in (MLA_Q_RANK, MLA_KV_RANK, MLA_ROPE, GLA_K_ALL, GLA_K_ALL, GLA_V_ALL, GLA_GATE_RANK, GLA_V_ALL):
        cols.append(w_in[:, o:o + n])
        o += n
    q_lat, kv_lat, kpe, gq, gk, gv, g_lr, og = cols
    misc_pad = jnp.zeros((D_MODEL, LANES - MLA_ROPE - GLA_GATE_RANK), w_in.dtype)
    w_in_r = jnp.concatenate([q_lat, kv_lat, gq, gk, gv, og, kpe, g_lr, misc_pad], axis=1).astype(BF16)

    wq = w_uq.reshape(MLA_Q_RANK, MLA_HEADS, MLA_QK)
    n_, r1, r2 = wq[..., :MLA_NOPE], wq[..., MLA_NOPE:MLA_NOPE + HALF_ROPE], wq[..., MLA_NOPE + HALF_ROPE:]
    w_uq_p = jnp.concatenate([n_, r1, r2, r2, r1], axis=-1).reshape(MLA_Q_RANK, MLA_HEADS * HEAD_PAD).astype(BF16)

    sc = MLA_QK ** -0.5 * math.log2(math.e)
    gn, g1, g2 = g_qn[:MLA_NOPE], g_qn[MLA_NOPE:MLA_NOPE + HALF_ROPE], g_qn[MLA_NOPE + HALF_ROPE:]
    z16, z64 = jnp.zeros((HALF_ROPE,), F32), jnp.zeros((MLA_NOPE,), F32)
    gq3 = jnp.stack([jnp.concatenate([gn, z16, z16, z16, z16]),
                     jnp.concatenate([z64, g1, g2, z16, z16]),
                     jnp.concatenate([z64, z16, z16, g2, g1])]) * sc
    kn, k1, k2 = g_kn[:MLA_NOPE], g_kn[MLA_NOPE:MLA_NOPE + HALF_ROPE], g_kn[MLA_NOPE + HALF_ROPE:]
    gk3 = jnp.stack([jnp.concatenate([kn, z16, z16, z16, z16]),
                     jnp.concatenate([z64, k1, k2, -k1, k2]),
                     jnp.concatenate([z64, -k2, k1, k2, k1])])
    eye = jnp.eye(HALF_ROPE, dtype=F32)
    zz = jnp.zeros((HALF_ROPE, HALF_ROPE), F32)
    pick1 = jnp.concatenate([eye, zz], axis=0)
    pick2 = jnp.concatenate([zz, eye], axis=0)
    z_n = jnp.zeros((MLA_ROPE, MLA_NOPE), F32)
    sel = jnp.concatenate([z_n, pick1, pick2, pick1, pick2, z_n, pick2, pick1, pick2, pick1], axis=1).astype(BF16)

    w_a2_p = jnp.zeros((LANES, GLA_K_ALL), F32).at[MLA_ROPE:MLA_ROPE + GLA_GATE_RANK].set(w_a2).astype(BF16)

    wo = w_out[:MLA_HEADS * MLA_V].reshape(MLA_HEADS, MLA_V, D_MODEL)
    w_out_mla = jnp.concatenate([jnp.zeros_like(wo), wo], axis=1).reshape(MLA_HEADS * HEAD_PAD, D_MODEL).astype(BF16)
    return {
        "score_bound": 1.01 * sc * MLA_QK * jnp.max(jnp.abs(g_qn)) * jnp.max(jnp.abs(g_kn)),
        "g_norm1": g_norm1.reshape(1, -1), "w_in": w_in_r, "g_qa": g_qa.reshape(1, -1), "w_uq": w_uq_p,
        "gq3": gq3, "g_kva": g_kva.reshape(1, -1), "w_ukv": w_ukv.astype(BF16), "sel": sel, "gk3": gk3,
        "w_a2": w_a2_p, "b_a2": b_a2.reshape(1, -1), "g_gla": g_gla.reshape(1, -1),
        "w_out_mla": w_out_mla, "w_out_gla": w_out[MLA_HEADS * MLA_V:].astype(BF16),
        "g_norm2": g_norm2.reshape(1, -1), "w_up": w_up.astype(BF16), "w_conv": w_conv,
        "b_conv": b_conv.reshape(1, -1), "w_down": w_down.astype(BF16),
    }


def _rope_tables(n):
    inv = 1.0 / (ROPE_THETA ** (jnp.arange(HALF_ROPE, dtype=F32) / HALF_ROPE))
    ang = jnp.arange(n, dtype=F32)[:, None] * jnp.tile(inv, LANES // HALF_ROPE)[None, :]
    return jnp.cos(ang), jnp.sin(ang)


def _state_to_t(s):
    b = s.shape[0]
    return s.reshape(b, GLA_K_ALL, GLA_DV).transpose(0, 2, 1)


def _state_from_t(st):
    b = st.shape[0]
    return st.transpose(0, 2, 1).reshape(b, GLA_HEADS, GLA_DK, GLA_DV)


def _pick_tile(n, pref):
    t = min(n, pref)
    while n % t:
        t //= 2
    return t


def _layer(x, mods, w, cos_t, sin_t, cache_ckv, cache_kpe, state, hist):
    b, l, _ = x.shape
    tm = _pick_tile(l, 512)
    pos0 = 0 if cache_ckv is None else cache_ckv.shape[1]
    ckv, kpe, q, gq, gk, gv, lg, og = _inproj(x, mods, w, cos_t, sin_t, tm, pos0)
    if cache_ckv is None:
        k, vt = _keys(ckv, kpe, w, cos_t, sin_t, tm, True)
        o_mla = _attn_prompt(q, k, vt, tm, w["score_bound"])
    else:
        n_valid = pos0 + l
        lk = -(-n_valid // LANES) * LANES
        ckv_all = jnp.concatenate([cache_ckv, ckv, jnp.zeros((b, lk - n_valid, MLA_KV_RANK), F32)], axis=1)
        kpe_all = jnp.concatenate([cache_kpe, kpe, jnp.zeros((b, lk - n_valid, MLA_ROPE), F32)], axis=1)
        k, v = _keys(ckv_all, kpe_all, w, cos_t, sin_t, lk, False)
        o_mla = _attn_sample(q, k, v, n_valid)
    o_gla, st = _gla(gq, gk, gv, lg, og, _state_to_t(state), w, tm)
    y, new_hist = _ffn(x, o_mla, o_gla, mods, hist, w, tm)
    return y, ckv, kpe, _state_from_t(st), new_hist


def kernel(x_prompt, x_sample, c_prompt, c_sample, cache_ckv, cache_kpe, state_gla, state_ffn_conv, w_ada, b_ada, g_norm1, w_in, g_qa, w_uq, g_qn, g_kva, w_ukv, g_kn, w_a2, b_a2, g_gla, w_out, g_norm2, w_up, w_conv, b_conv, w_down):
    depth = w_ada.shape[0]
    bp, lp, _ = x_prompt.shape
    bs, ls, _ = x_sample.shape
    n_pos = max(lp, cache_ckv.shape[2] + ls)
    n_pos = -(-n_pos // LANES) * LANES
    cos_t, sin_t = _rope_tables(n_pos)
    yp, ys = x_prompt, x_sample
    outs = [[] for _ in range(8)]
    for i in range(depth):
        w = _prep_weights(w_in[i], g_norm1[i], g_qa[i], w_uq[i], g_qn[i], g_kva[i], w_ukv[i], g_kn[i],
                          w_a2[i], b_a2[i], g_gla[i], w_out[i], g_norm2[i], w_up[i], w_conv[i], b_conv[i],
                          w_down[i])
        mods = _adaln(jnp.concatenate([c_prompt, c_sample], axis=0), w_ada[i], b_ada[i])
        mods = mods.reshape(bp + bs, 6, D_MODEL)
        zero_state = jnp.zeros((bp, GLA_HEADS, GLA_DK, GLA_DV), F32)
        zero_hist = jnp.zeros((bp, CONV_W - 1, FFN_DIM), F32)
        yp, a, b_, s, h = _layer(yp, mods[:bp], w, cos_t, sin_t, None, None, zero_state, zero_hist)
        for lst, val in zip(outs[:4], (a, b_, s, h)):
            lst.append(val)
        ys, a, b_, s, h = _layer(ys, mods[bp:], w, cos_t, sin_t, cache_ckv[i], cache_kpe[i],
                                 state_gla[i], state_ffn_conv[i])
        for lst, val in zip(outs[4:], (a, b_, s, h)):
            lst.append(val)
    return (yp, ys) + tuple(jnp.stack(o) for o in outs)
```

```python
import functools
import math

import jax
import jax.numpy as jnp
from jax import lax
from jax.experimental import pallas as pl
from jax.experimental.pallas import tpu as pltpu

F32 = jnp.float32
BF16 = jnp.bfloat16

D_MODEL = 1024
CHUNK = 64
CHUNK_SHIFT = 6
MLA_HEADS = 8
MLA_NOPE = 64
MLA_ROPE = 32
MLA_QK = MLA_NOPE + MLA_ROPE
MLA_V = 64
MLA_Q_RANK = 384
MLA_KV_RANK = 256
ROPE_THETA = 10000.0
GLA_HEADS = 4
GLA_DK = 64
GLA_DV = 128
GLA_GATE_RANK = 16
GLA_TAU = 16.0
FFN_DIM = 2816
CONV_W = 3
EPS = 1e-6

LANES = 128
HEAD_PAD = LANES
HALF_ROPE = MLA_ROPE // 2
GLA_K_ALL = GLA_HEADS * GLA_DK
GLA_V_ALL = GLA_HEADS * GLA_DV
C_QLAT = 0
C_KV = C_QLAT + MLA_Q_RANK
C_GQ = C_KV + MLA_KV_RANK
C_GK = C_GQ + GLA_K_ALL
C_GV = C_GK + GLA_K_ALL
C_OG = C_GV + GLA_V_ALL
C_MISC = C_OG + GLA_V_ALL
IN_COLS_PAD = C_MISC + LANES
FFN_TILE = 256
EXP_CLAMP = 80.0
VMEM_LIMIT = 56 * 1024 * 1024


def _const_spec(shape):
    nd = len(shape)
    return pl.BlockSpec(shape, lambda *_: (0,) * nd, pipeline_mode=pl.Buffered(1))


def _params(*sem):
    return pltpu.CompilerParams(dimension_semantics=sem, vmem_limit_bytes=VMEM_LIMIT)


def _rms_rows(x):
    return x * lax.rsqrt(jnp.mean(x * x, axis=-1, keepdims=True) + EPS)


def _ada_kernel(c_ref, w_ref, b_ref, o_ref):
    c = c_ref[...]
    s = c * jax.nn.sigmoid(c)
    o_ref[...] = jnp.dot(s, w_ref[...], precision=lax.Precision.HIGHEST,
                         preferred_element_type=F32) + b_ref[...]


def _adaln(c, w_ada, b_ada):
    n, d = c.shape
    cols = w_ada.shape[1]
    tn = 1024
    return pl.pallas_call(
        _ada_kernel,
        grid=(cols // tn,),
        in_specs=[pl.BlockSpec((n, d), lambda j: (0, 0)),
                  pl.BlockSpec((d, tn), lambda j: (0, j)),
                  pl.BlockSpec((1, tn), lambda j: (0, j))],
        out_specs=pl.BlockSpec((n, tn), lambda j: (0, j)),
        out_shape=jax.ShapeDtypeStruct((n, cols), F32),
        compiler_params=_params("arbitrary"),
        name="adaln",
    )(c, w_ada, b_ada.reshape(1, cols))


def _inproj_kernel(x_ref, mods_ref, g1_ref, win_ref, gqa_ref, wuq_ref, gq3_ref, cos_ref, sin_ref,
                   gkva_ref, wa2_ref, ba2_ref,
                   ckv_ref, kpe_ref, q_ref, gq_ref, gk_ref, gv_ref, lg_ref, og_ref):
    x = x_ref[0]
    shift = mods_ref[0, 0:1, :]
    scale = mods_ref[0, 1:2, :]
    h = _rms_rows(x) * g1_ref[...] * (1.0 + scale) + shift
    p = jnp.dot(h.astype(BF16), win_ref[...], preferred_element_type=F32)

    ckv_ref[0] = _rms_rows(p[:, C_KV:C_GQ]) * gkva_ref[...]
    misc = p[:, C_MISC:IN_COLS_PAD]
    kpe_ref[0] = misc[:, 0:MLA_ROPE]

    qa = _rms_rows(p[:, C_QLAT:C_KV]) * gqa_ref[...]
    qu = jnp.dot(qa.astype(BF16), wuq_ref[...], preferred_element_type=F32)
    tq = gq3_ref[0:1, :] + cos_ref[...] * gq3_ref[1:2, :] + sin_ref[...] * gq3_ref[2:3, :]
    lane = lax.broadcasted_iota(jnp.int32, (1, HEAD_PAD), 1)
    real = (lane < MLA_QK).astype(F32)
    for hd in range(MLA_HEADS):
        xh = qu[:, hd * HEAD_PAD:(hd + 1) * HEAD_PAD]
        ss = jnp.sum(xh * xh * real, axis=-1, keepdims=True)
        r = lax.rsqrt(ss * (1.0 / MLA_QK) + EPS)
        q_ref[0, :, hd * HEAD_PAD:(hd + 1) * HEAD_PAD] = (xh * r * tq).astype(BF16)

    gq_ref[0] = (p[:, C_GQ:C_GK] * (GLA_DK ** -0.5)).astype(BF16)
    gk_ref[0] = p[:, C_GK:C_GV].astype(BF16)
    gv_ref[0] = p[:, C_GV:C_OG].astype(BF16)
    og_ref[0] = p[:, C_OG:C_MISC].astype(BF16)
    z = jnp.dot(misc.astype(BF16), wa2_ref[...], preferred_element_type=F32) + ba2_ref[...]
    log_sig = jnp.minimum(z, 0.0) - jnp.log1p(jnp.exp(-jnp.abs(z)))
    lg_ref[0] = log_sig * (1.0 / GLA_TAU)


def _inproj(x, mods, w, cos_t, sin_t, tm, pos0):
    b, l, d = x.shape
    nt = l // tm
    p0 = pos0 // tm
    row = lambda bi, li: (bi, li, 0)
    tab = lambda bi, li: (li + p0, 0)
    outs = [(MLA_KV_RANK, F32), (MLA_ROPE, F32), (MLA_HEADS * HEAD_PAD, BF16), (GLA_K_ALL, BF16),
            (GLA_K_ALL, BF16), (GLA_V_ALL, BF16), (GLA_K_ALL, F32), (GLA_V_ALL, BF16)]
    return pl.pallas_call(
        _inproj_kernel,
        grid=(b, nt),
        in_specs=[pl.BlockSpec((1, tm, d), row),
                  pl.BlockSpec((1, 6, d), lambda bi, li: (bi, 0, 0)),
                  _const_spec((1, d)),
                  _const_spec((d, IN_COLS_PAD)),
                  _const_spec((1, MLA_Q_RANK)),
                  _const_spec((MLA_Q_RANK, MLA_HEADS * HEAD_PAD)),
                  _const_spec((3, HEAD_PAD)),
                  pl.BlockSpec((tm, HEAD_PAD), tab),
                  pl.BlockSpec((tm, HEAD_PAD), tab),
                  _const_spec((1, MLA_KV_RANK)),
                  _const_spec((LANES, GLA_K_ALL)),
                  _const_spec((1, GLA_K_ALL))],
        out_specs=[pl.BlockSpec((1, tm, n), row) for n, _ in outs],
        out_shape=[jax.ShapeDtypeStruct((b, l, n), dt) for n, dt in outs],
        compiler_params=_params("arbitrary", "arbitrary"),
        name="inproj",
    )(x, mods, w["g_norm1"], w["w_in"], w["g_qa"], w["w_uq"], w["gq3"], cos_t, sin_t,
      w["g_kva"], w["w_a2"], w["b_a2"])


def _keys_kernel(ckv_ref, kpe_ref, wukv_ref, wuvt_ref, sel_ref, gk3_ref, cos_ref, sin_ref, k_ref, v_ref, *,
                 transpose_v):
    ckv = ckv_ref[0].astype(BF16)
    kv = jnp.dot(ckv, wukv_ref[...], preferred_element_type=F32)
    if transpose_v:
        vt_all = lax.dot_general(wuvt_ref[...], ckv, _NT, preferred_element_type=F32)
        ones_rows = jnp.ones((MLA_V, ckv.shape[0]), BF16)
    kpe = kpe_ref[0]
    kpe_hi = kpe.astype(BF16)
    kpe_lo = (kpe - kpe_hi.astype(F32)).astype(BF16)
    uv = (jnp.dot(kpe_hi, sel_ref[...], preferred_element_type=F32)
          + jnp.dot(kpe_lo, sel_ref[...], preferred_element_type=F32))
    rot = (uv[:, :HEAD_PAD] * (cos_ref[...] * gk3_ref[1:2, :])
           + uv[:, HEAD_PAD:] * (sin_ref[...] * gk3_ref[2:3, :]))
    sp = jnp.sum(kpe * kpe, axis=-1, keepdims=True)
    lane = lax.broadcasted_iota(jnp.int32, (1, HEAD_PAD), 1)
    is_nope = lane < MLA_NOPE
    g_nope = gk3_ref[0:1, :]
    for hd in range(MLA_HEADS):
        blk = kv[:, hd * HEAD_PAD:(hd + 1) * HEAD_PAD]
        kn = jnp.where(is_nope, blk, 0.0)
        ss = jnp.sum(kn * kn, axis=-1, keepdims=True) + sp
        r = lax.rsqrt(ss * (1.0 / MLA_QK) + EPS)
        k_ref[0, :, hd * HEAD_PAD:(hd + 1) * HEAD_PAD] = ((kn * g_nope + rot) * r).astype(BF16)
        if transpose_v:
            v_at = hd * HEAD_PAD + (hd % 2) * MLA_V
            ones_at = hd * HEAD_PAD + (1 - hd % 2) * MLA_V
            v_ref[0, ones_at:ones_at + MLA_V, :] = ones_rows
            v_ref[0, v_at:v_at + MLA_V, :] = vt_all[hd * MLA_V:(hd + 1) * MLA_V, :].astype(BF16)
        else:
            v_ref[0, :, hd * HEAD_PAD:(hd + 1) * HEAD_PAD] = jnp.where(is_nope, 1.0, blk).astype(BF16)


def _keys(ckv, kpe, w, cos_t, sin_t, tm, transpose_v):
    b, l, _ = ckv.shape
    row = lambda bi, li: (bi, li, 0)
    tab = lambda bi, li: (li, 0)
    hw = MLA_HEADS * HEAD_PAD
    if transpose_v:
        v_spec = pl.BlockSpec((1, hw, tm), lambda bi, li: (bi, 0, li))
        v_shape = jax.ShapeDtypeStruct((b, hw, l), BF16)
    else:
        v_spec = pl.BlockSpec((1, tm, hw), row)
        v_shape = jax.ShapeDtypeStruct((b, l, hw), BF16)
    return pl.pallas_call(
        functools.partial(_keys_kernel, transpose_v=transpose_v),
        grid=(b, l // tm),
        in_specs=[pl.BlockSpec((1, tm, MLA_KV_RANK), row),
                  pl.BlockSpec((1, tm, MLA_ROPE), row),
                  _const_spec((MLA_KV_RANK, hw)),
                  _const_spec((MLA_HEADS * MLA_V, MLA_KV_RANK)),
                  _const_spec((MLA_ROPE, 2 * HEAD_PAD)),
                  _const_spec((3, HEAD_PAD)),
                  pl.BlockSpec((tm, HEAD_PAD), tab),
                  pl.BlockSpec((tm, HEAD_PAD), tab)],
        out_specs=[pl.BlockSpec((1, tm, hw), row), v_spec],
        out_shape=[jax.ShapeDtypeStruct((b, l, hw), BF16), v_shape],
        compiler_params=_params("arbitrary", "arbitrary"),
        name="keys",
    )(ckv, kpe, w["w_ukv"], w["w_uv_t"], w["sel"], w["gk3"], cos_t, sin_t)


_NT = (((1,), (1,)), ((), ()))
_TN = (((0,), (0,)), ((), ()))
MASKED = -1e30
EXP2_SAFE = 100.0


def _store_normalised(tiles, o_ref):
    odd = pl.program_id(1) % 2

    for half in range(2):
        @pl.when(odd == half)
        def _():
            lanes = slice(half * MLA_V, (half + 1) * MLA_V)
            for acc_ref, rows in tiles:
                acc = acc_ref[...]
                den = acc[(1 - half) * MLA_V:(1 - half) * MLA_V + 1, :]
                o_ref[0, rows, lanes] = (acc / den).T.astype(BF16)[:, lanes]


def _finalize_tile(acc_scr, o_ref, i, tq):
    _store_normalised([(acc_scr, pl.ds(pl.multiple_of(i * tq, tq), tq))], o_ref)


def _chunk_visible(tq):
    kc = lax.broadcasted_iota(jnp.int32, (tq, tq), 0) >> CHUNK_SHIFT
    qc = lax.broadcasted_iota(jnp.int32, (tq, tq), 1) >> CHUNK_SHIFT
    return kc <= qc


ATTN_LAG = 2
ATTN_UNROLL = 6


def _attn_bounded_kernel(iq_tab, ia_tab, j_tab, d_tab, q_ref, k_ref, vt_ref, mask_ref, o_ref, p_buf, acc_scr,
                         *, tq, nq, n_iter):
    p_buf[...] = jnp.zeros_like(p_buf)
    acc_scr[...] = jnp.zeros_like(acc_scr)
    n_slots = ATTN_LAG + 1

    def sub_step(t, slot):
        tp = jnp.maximum(t - ATTN_LAG, 0)
        jp = j_tab[tp]
        vt = vt_ref[0, :, pl.ds(pl.multiple_of(jp * tq, tq), tq)]
        cols = pl.ds(pl.multiple_of(ia_tab[tp] * tq, tq), tq)
        acc_scr[:, cols] = (jnp.where(jp == 0, 0.0, acc_scr[:, cols])
                            + jnp.dot(vt, p_buf[(slot + 1) % n_slots], preferred_element_type=F32))
        kb = k_ref[0, pl.ds(pl.multiple_of(j_tab[t] * tq, tq), tq), :]
        qt = q_ref[0, pl.ds(pl.multiple_of(iq_tab[t] * tq, tq), tq), :]
        p = jnp.exp2(lax.dot_general(kb, qt, _NT, preferred_element_type=F32)).astype(BF16)
        p_buf[slot] = p * mask_ref[d_tab[t]]

    def body(it, carry):
        for u in range(ATTN_UNROLL):
            sub_step(ATTN_UNROLL * it + u, u % n_slots)
        return carry

    lax.fori_loop(0, n_iter, body, 0)
    _store_normalised([(acc_scr.at[:, i * tq:(i + 1) * tq], slice(i * tq, (i + 1) * tq)) for i in range(nq)],
                      o_ref)


def _attn_online_kernel(i_tab, j_tab, q_ref, k_ref, vt_ref, o_ref, s_buf, p_buf, a_buf, m_scr, acc_scr,
                        *, tq, n_steps, n_iter):
    s_buf[...] = jnp.zeros_like(s_buf)
    p_buf[...] = jnp.zeros_like(p_buf)
    a_buf[...] = jnp.zeros_like(a_buf)
    m_scr[...] = jnp.zeros_like(m_scr)
    acc_scr[...] = jnp.zeros_like(acc_scr)

    def sub_step(t, slot):
        other = 1 - slot
        t2 = jnp.maximum(t - 2, 0)
        i2, j2 = i_tab[t2], j_tab[t2]
        vt = vt_ref[0, :, pl.ds(pl.multiple_of(j2 * tq, tq), tq)]
        acc_scr[...] = a_buf[slot] * acc_scr[...] + jnp.dot(vt, p_buf[slot], preferred_element_type=F32)
        t1 = jnp.maximum(t - 1, 0)
        s = s_buf[other]
        m_old = jnp.where(j_tab[t1] == 0, -jnp.inf, m_scr[...])
        m_new = jnp.maximum(m_old, jnp.max(s, axis=0, keepdims=True))
        a_buf[other] = jnp.exp2(m_old - m_new)
        p_buf[other] = jnp.exp2(s - m_new).astype(BF16)
        m_scr[...] = m_new
        t0 = jnp.minimum(t, n_steps - 1)
        i0, j0 = i_tab[t0], j_tab[t0]
        kb = k_ref[0, pl.ds(pl.multiple_of(j0 * tq, tq), tq), :]
        qt = q_ref[0, pl.ds(pl.multiple_of(i0 * tq, tq), tq), :]
        s_buf[slot] = lax.dot_general(kb, qt, _NT, preferred_element_type=F32)

        @pl.when(i0 == j0)
        def _():
            s_buf[slot] = jnp.where(_chunk_visible(tq), s_buf[slot], MASKED)

        @pl.when(jnp.logical_and(jnp.logical_and(t >= 2, t - 2 < n_steps), i2 == j2))
        def _():
            _finalize_tile(acc_scr, o_ref, i2, tq)

    def body(it, carry):
        sub_step(2 * it, 0)
        sub_step(2 * it + 1, 1)
        return carry

    lax.fori_loop(0, n_iter, body, 0)


def _attn_specs(b, l, n_tabs, extra_in=()):
    qmap = lambda bi, hi, *_: (bi, 0, hi)
    return dict(
        num_scalar_prefetch=n_tabs,
        grid=(b, MLA_HEADS),
        in_specs=[pl.BlockSpec((1, l, HEAD_PAD), qmap),
                  pl.BlockSpec((1, l, HEAD_PAD), qmap),
                  pl.BlockSpec((1, HEAD_PAD, l), lambda bi, hi, *_: (bi, hi, 0)), *extra_in],
        out_specs=pl.BlockSpec((1, l, 2 * MLA_V), lambda bi, hi, *_: (bi, 0, hi // 2)))


def _attn_online_call(q, k, vt, tq):
    b, l, hw = q.shape
    nq = l // tq
    pairs = [(i, j) for i in range(nq) for j in range(i + 1)]
    n_steps = len(pairs)
    n_iter = (n_steps + 3) // 2
    pairs = pairs + [pairs[-1]] * (2 * n_iter - n_steps)
    i_tab = jnp.asarray([p_[0] for p_ in pairs], jnp.int32)
    j_tab = jnp.asarray([p_[1] for p_ in pairs], jnp.int32)
    grid_spec = pltpu.PrefetchScalarGridSpec(
        **_attn_specs(b, l, 2),
        scratch_shapes=[pltpu.VMEM((2, tq, tq), F32), pltpu.VMEM((2, tq, tq), BF16),
                        pltpu.VMEM((2, 1, tq), F32), pltpu.VMEM((1, tq), F32),
                        pltpu.VMEM((HEAD_PAD, tq), F32)])
    return pl.pallas_call(
        functools.partial(_attn_online_kernel, tq=tq, n_steps=n_steps, n_iter=n_iter),
        grid_spec=grid_spec,
        out_shape=jax.ShapeDtypeStruct((b, l, MLA_HEADS * MLA_V), BF16),
        compiler_params=_params("arbitrary", "arbitrary"),
        name="attn_online",
    )(i_tab, j_tab, q, k, vt)


def _attn_bounded_call(q, k, vt, tq):
    b, l, hw = q.shape
    nq = l // tq
    pairs = [(i, j) for i in range(nq) for j in range(i + 1)]
    n_iter = -(-(len(pairs) + ATTN_LAG) // ATTN_UNROLL)
    n_dummy = n_iter * ATTN_UNROLL - len(pairs)
    iq_tab = jnp.asarray([p_[0] for p_ in pairs] + [nq - 1] * n_dummy, jnp.int32)
    ia_tab = jnp.asarray([p_[0] for p_ in pairs] + [nq] * n_dummy, jnp.int32)
    j_tab = jnp.asarray([p_[1] for p_ in pairs] + [0] * n_dummy, jnp.int32)
    d_tab = jnp.asarray([int(p_[0] == p_[1]) for p_ in pairs] + [0] * n_dummy, jnp.int32)
    masks = jnp.stack([jnp.ones((tq, tq), BF16), _chunk_visible(tq).astype(BF16)])
    grid_spec = pltpu.PrefetchScalarGridSpec(
        **_attn_specs(b, l, 4, [pl.BlockSpec((2, tq, tq), lambda *_: (0, 0, 0), pipeline_mode=pl.Buffered(1))]),
        scratch_shapes=[pltpu.VMEM((ATTN_LAG + 1, tq, tq), BF16), pltpu.VMEM((HEAD_PAD, (nq + 1) * tq), F32)])
    return pl.pallas_call(
        functools.partial(_attn_bounded_kernel, tq=tq, nq=nq, n_iter=n_iter),
        grid_spec=grid_spec,
        out_shape=jax.ShapeDtypeStruct((b, l, MLA_HEADS * MLA_V), BF16),
        compiler_params=_params("arbitrary", "arbitrary"),
        name="attn_bounded",
    )(iq_tab, ia_tab, j_tab, d_tab, q, k, vt, masks)


def _attn_prompt(q, k, vt, tq, score_bound):
    return lax.cond(score_bound <= EXP2_SAFE,
                    functools.partial(_attn_bounded_call, tq=tq),
                    functools.partial(_attn_online_call, tq=tq),
                    q, k, vt)


def _attn_sample_kernel(q_ref, k_ref, v_ref, o_ref, *, n_valid):
    outs = []
    for hd in range(MLA_HEADS):
        cols = slice(hd * HEAD_PAD, (hd + 1) * HEAD_PAD)
        s = lax.dot_general(q_ref[0, :, cols], k_ref[0, :, cols], _NT, preferred_element_type=F32)
        col = lax.broadcasted_iota(jnp.int32, s.shape, 1)
        s = jnp.where(col < n_valid, s, MASKED)
        p = jnp.exp2(s - jnp.max(s, axis=-1, keepdims=True))
        acc = jnp.dot(p.astype(BF16), v_ref[0, :, cols], preferred_element_type=F32)
        outs.append(acc / acc[:, 0:1])
    lane = lax.broadcasted_iota(jnp.int32, (1, HEAD_PAD), 1)
    for pair in range(MLA_HEADS // 2):
        even = pltpu.roll(outs[2 * pair], MLA_V, 1)
        o_ref[0, :, pair * HEAD_PAD:(pair + 1) * HEAD_PAD] = (
            jnp.where(lane < MLA_V, even, outs[2 * pair + 1]).astype(BF16))


def _attn_sample(q, k, v, n_valid):
    b, t, hw = q.shape
    lk = k.shape[1]
    bmap = lambda bi: (bi, 0, 0)
    return pl.pallas_call(
        functools.partial(_attn_sample_kernel, n_valid=n_valid),
        grid=(b,),
        in_specs=[pl.BlockSpec((1, t, hw), bmap),
                  pl.BlockSpec((1, lk, hw), bmap),
                  pl.BlockSpec((1, lk, hw), bmap)],
        out_specs=pl.BlockSpec((1, t, MLA_HEADS * MLA_V), bmap),
        out_shape=jax.ShapeDtypeStruct((b, t, MLA_HEADS * MLA_V), BF16),
        compiler_params=_params("arbitrary"),
        name="attn_sample",
    )(q, k, v)


def _gla_kernel(gq_ref, gk_ref, gv_ref, lg_ref, og_ref, tri_ref, ggla_ref, st0_ref,
                o_ref, stf_ref, st_scr, b_scr, *, tg, nb):
    li = pl.program_id(1)

    @pl.when(li == 0)
    def _():
        st_scr[...] = st0_ref[...]

    tri = tri_ref[...]
    for bb in range(nb):
        lg = lg_ref[bb]
        lg_hi = lg.astype(BF16)
        lg_lo = (lg - lg_hi.astype(F32)).astype(BF16)
        b_scr[bb] = (jnp.dot(tri, lg_hi, preferred_element_type=F32)
                     + jnp.dot(tri, lg_lo, preferred_element_type=F32))

    lane = lax.broadcasted_iota(jnp.int32, (1, GLA_K_ALL), 1)
    head_of_lane = lane >> CHUNK_SHIFT
    ri = lax.broadcasted_iota(jnp.int32, (GLA_HEADS * CHUNK, CHUNK), 0) & (CHUNK - 1)
    cj = lax.broadcasted_iota(jnp.int32, (GLA_HEADS * CHUNK, CHUNK), 1)
    causal = cj <= ri
    g_out = ggla_ref[...]

    def chunk(bb, r0):
        b = b_scr[bb, r0:r0 + CHUNK, :]
        b_mid = b[CHUNK // 2 - 1:CHUNK // 2, :]
        b_last = b[CHUNK - 1:CHUNK, :]
        q = gq_ref[bb, r0:r0 + CHUNK, :].astype(F32)
        k = gk_ref[bb, r0:r0 + CHUNK, :].astype(F32)
        qe = q * jnp.exp(jnp.minimum(b - b_mid, EXP_CLAMP))
        ke = (k * jnp.exp(jnp.minimum(b_mid - b, EXP_CLAMP))).astype(BF16)
        qb = q * jnp.exp(b)
        kd = k * jnp.exp(b_last - b)
        zero = jnp.zeros_like(q)
        qe_st = jnp.concatenate([jnp.where(head_of_lane == hd, qe, zero) for hd in range(GLA_HEADS)],
                                axis=0).astype(BF16)
        qb_st = jnp.concatenate([jnp.where(head_of_lane == hd, qb, zero) for hd in range(GLA_HEADS)],
                                axis=0).astype(BF16)
        a_st = lax.dot_general(qe_st, ke, _NT, preferred_element_type=F32)
        a_st = jnp.where(causal, a_st, 0.0).astype(BF16)
        st = st_scr[bb]
        o_inter = lax.dot_general(qb_st, st.astype(BF16), _NT, preferred_element_type=F32)
        upd = jnp.zeros_like(st)
        for hd in range(GLA_HEADS):
            vh = gv_ref[bb, r0:r0 + CHUNK, hd * GLA_DV:(hd + 1) * GLA_DV]
            o = (o_inter[hd * CHUNK:(hd + 1) * CHUNK, :]
                 + jnp.dot(a_st[hd * CHUNK:(hd + 1) * CHUNK, :], vh, preferred_element_type=F32))
            og = og_ref[bb, r0:r0 + CHUNK, hd * GLA_DV:(hd + 1) * GLA_DV].astype(F32)
            o = _rms_rows(o) * g_out * (og * jax.nn.sigmoid(og))
            o_ref[bb, r0:r0 + CHUNK, hd * GLA_DV:(hd + 1) * GLA_DV] = o.astype(BF16)
            kd_h = jnp.where(head_of_lane == hd, kd, zero).astype(BF16)
            upd = upd + lax.dot_general(vh, kd_h, _TN, preferred_element_type=F32)
        st_scr[bb] = st * jnp.exp(b_last) + upd

    for c in range(tg // CHUNK):
        for bb in range(nb):
            chunk(bb, c * CHUNK)

    stf_ref[...] = st_scr[...]


def _gla(gq, gk, gv, lg, og, st0, w, tg, nb):
    b, l, _ = gq.shape
    row = lambda bi, li: (bi, li, 0)
    fix = lambda bi, li: (bi, 0, 0)
    blk = jnp.arange(tg) // CHUNK
    tri = ((blk[:, None] == blk[None, :]) & (jnp.arange(tg)[None, :] <= jnp.arange(tg)[:, None])).astype(BF16)
    return pl.pallas_call(
        functools.partial(_gla_kernel, tg=tg, nb=nb),
        grid=(b // nb, l // tg),
        in_specs=[pl.BlockSpec((nb, tg, GLA_K_ALL), row),
                  pl.BlockSpec((nb, tg, GLA_K_ALL), row),
                  pl.BlockSpec((nb, tg, GLA_V_ALL), row),
                  pl.BlockSpec((nb, tg, GLA_K_ALL), row),
                  pl.BlockSpec((nb, tg, GLA_V_ALL), row),
                  _const_spec((tg, tg)),
                  _const_spec((1, GLA_DV)),
                  pl.BlockSpec((nb, GLA_DV, GLA_K_ALL), fix)],
        out_specs=[pl.BlockSpec((nb, tg, GLA_V_ALL), row),
                   pl.BlockSpec((nb, GLA_DV, GLA_K_ALL), fix)],
        out_shape=[jax.ShapeDtypeStruct((b, l, GLA_V_ALL), BF16),
                   jax.ShapeDtypeStruct((b, GLA_DV, GLA_K_ALL), F32)],
        scratch_shapes=[pltpu.VMEM((nb, GLA_DV, GLA_K_ALL), F32), pltpu.VMEM((nb, tg, GLA_K_ALL), F32)],
        compiler_params=_params("arbitrary", "arbitrary"),
        name="gla",
    )(gq, gk, gv, lg, og, tri, w["g_gla"], st0)


def _ffn_kernel(x_ref, om_ref, ogl_ref, mods_ref, g2_ref, wom_ref, wog_ref, wup_ref, wcv_ref, bcv_ref,
                wdn_ref, hist_ref, y_ref, nh_ref, carry_scr, a_scr, act_scr, *, tm):
    li = pl.program_id(1)
    hist_rows = CONV_W - 1
    pad = 8

    @pl.when(li == 0)
    def _():
        carry_scr[...] = jnp.zeros_like(carry_scr)
        carry_scr[pad - hist_rows:pad, :] = hist_ref[0]

    mixed = (jnp.dot(om_ref[0], wom_ref[...], preferred_element_type=F32)
             + jnp.dot(ogl_ref[0], wog_ref[...], preferred_element_type=F32))
    x1 = x_ref[0] + mods_ref[0, 2:3, :] * mixed
    h = (_rms_rows(x1) * g2_ref[...] * (1.0 + mods_ref[0, 4:5, :]) + mods_ref[0, 3:4, :]).astype(BF16)

    for f in range(FFN_DIM // FFN_TILE):
        c0 = f * FFN_TILE
        a = jnp.dot(h, wup_ref[:, c0:c0 + FFN_TILE], preferred_element_type=F32)
        g = jnp.dot(h, wup_ref[:, FFN_DIM + c0:FFN_DIM + c0 + FFN_TILE], preferred_element_type=F32)
        a_scr[0:pad, :] = carry_scr[:, c0:c0 + FFN_TILE]
        a_scr[pad:pad + tm, :] = a
        carry_scr[:, c0:c0 + FFN_TILE] = a[tm - pad:tm, :]
        nh_ref[0, :, c0:c0 + FFN_TILE] = a[tm - hist_rows:tm, :]
        conv = (bcv_ref[:, c0:c0 + FFN_TILE]
                + wcv_ref[2:3, c0:c0 + FFN_TILE] * a
                + wcv_ref[1:2, c0:c0 + FFN_TILE] * a_scr[pad - 1:pad - 1 + tm, :]
                + wcv_ref[0:1, c0:c0 + FFN_TILE] * a_scr[pad - 2:pad - 2 + tm, :])
        act = (jax.nn.gelu(conv) * g).astype(BF16)
        act_scr[:, c0:c0 + FFN_TILE] = act

    y = jnp.dot(act_scr[...], wdn_ref[...], preferred_element_type=F32)
    y_ref[0] = x1 + mods_ref[0, 5:6, :] * y


def _ffn(x, o_mla, o_gla, mods, hist, w, tm):
    b, l, d = x.shape
    row = lambda bi, li: (bi, li, 0)
    fix = lambda bi, li: (bi, 0, 0)
    hw = MLA_HEADS * MLA_V
    return pl.pallas_call(
        functools.partial(_ffn_kernel, tm=tm),
        grid=(b, l // tm),
        in_specs=[pl.BlockSpec((1, tm, d), row),
                  pl.BlockSpec((1, tm, hw), row),
                  pl.BlockSpec((1, tm, GLA_V_ALL), row),
                  pl.BlockSpec((1, 6, d), fix),
                  _const_spec((1, d)),
                  _const_spec((hw, d)),
                  _const_spec((GLA_V_ALL, d)),
                  _const_spec((d, 2 * FFN_DIM)),
                  _const_spec((CONV_W, FFN_DIM)),
                  _const_spec((1, FFN_DIM)),
                  _const_spec((FFN_DIM, d)),
                  pl.BlockSpec((1, CONV_W - 1, FFN_DIM), fix)],
        out_specs=[pl.BlockSpec((1, tm, d), row),
                   pl.BlockSpec((1, CONV_W - 1, FFN_DIM), fix)],
        out_shape=[jax.ShapeDtypeStruct((b, l, d), F32),
                   jax.ShapeDtypeStruct((b, CONV_W - 1, FFN_DIM), F32)],
        scratch_shapes=[pltpu.VMEM((8, FFN_DIM), F32),
                        pltpu.VMEM((tm + 8, FFN_TILE), F32),
                        pltpu.VMEM((tm, FFN_DIM), BF16)],
        compiler_params=_params("arbitrary", "arbitrary"),
        name="ffn",
    )(x, o_mla, o_gla, mods, w["g_norm2"], w["w_out_mla"], w["w_out_gla"], w["w_up"], w["w_conv"],
      w["b_conv"], w["w_down"], hist)


def _prep_weights(w_in, g_norm1, g_qa, w_uq, g_qn, g_kva, w_ukv, g_kn, w_a2, b_a2, g_gla, w_out,
                  g_norm2, w_up, w_conv, b_conv, w_down):
    o, cols = 0, []
    for n in (MLA_Q_RANK, MLA_KV_RANK, MLA_ROPE, GLA_K_ALL, GLA_K_ALL, GLA_V_ALL, GLA_GATE_RANK, GLA_V_ALL):
        cols.append(w_in[:, o:o + n])
        o += n
    q_lat, kv_lat, kpe, gq, gk, gv, g_lr, og = cols
    misc_pad = jnp.zeros((D_MODEL, LANES - MLA_ROPE - GLA_GATE_RANK), w_in.dtype)
    w_in_r = jnp.concatenate([q_lat, kv_lat, gq, gk, gv, og, kpe, g_lr, misc_pad], axis=1).astype(BF16)

    wq = w_uq.reshape(MLA_Q_RANK, MLA_HEADS, MLA_QK)
    n_, r1, r2 = wq[..., :MLA_NOPE], wq[..., MLA_NOPE:MLA_NOPE + HALF_ROPE], wq[..., MLA_NOPE + HALF_ROPE:]
    w_uq_p = jnp.concatenate([n_, r1, r2, r2, r1], axis=-1).reshape(MLA_Q_RANK, MLA_HEADS * HEAD_PAD).astype(BF16)

    sc = MLA_QK ** -0.5 * math.log2(math.e)
    gn, g1, g2 = g_qn[:MLA_NOPE], g_qn[MLA_NOPE:MLA_NOPE + HALF_ROPE], g_qn[MLA_NOPE + HALF_ROPE:]
    z16, z64 = jnp.zeros((HALF_ROPE,), F32), jnp.zeros((MLA_NOPE,), F32)
    gq3 = jnp.stack([jnp.concatenate([gn, z16, z16, z16, z16]),
                     jnp.concatenate([z64, g1, g2, z16, z16]),
                     jnp.concatenate([z64, z16, z16, g2, g1])]) * sc
    kn, k1, k2 = g_kn[:MLA_NOPE], g_kn[MLA_NOPE:MLA_NOPE + HALF_ROPE], g_kn[MLA_NOPE + HALF_ROPE:]
    gk3 = jnp.stack([jnp.concatenate([kn, z16, z16, z16, z16]),
                     jnp.concatenate([z64, k1, k2, -k1, k2]),
                     jnp.concatenate([z64, -k2, k1, k2, k1])])
    eye = jnp.eye(HALF_ROPE, dtype=F32)
    zz = jnp.zeros((HALF_ROPE, HALF_ROPE), F32)
    pick1 = jnp.concatenate([eye, zz], axis=0)
    pick2 = jnp.concatenate([zz, eye], axis=0)
    z_n = jnp.zeros((MLA_ROPE, MLA_NOPE), F32)
    sel = jnp.concatenate([z_n, pick1, pick2, pick1, pick2, z_n, pick2, pick1, pick2, pick1], axis=1).astype(BF16)

    w_a2_p = jnp.zeros((LANES, GLA_K_ALL), F32).at[MLA_ROPE:MLA_ROPE + GLA_GATE_RANK].set(w_a2).astype(BF16)

    w_out_mla = w_out[:MLA_HEADS * MLA_V].astype(BF16)
    return {
        "score_bound": 1.01 * sc * MLA_QK * jnp.max(jnp.abs(g_qn)) * jnp.max(jnp.abs(g_kn)),
        "g_norm1": g_norm1.reshape(1, -1), "w_in": w_in_r, "g_qa": g_qa.reshape(1, -1), "w_uq": w_uq_p,
        "gq3": gq3, "g_kva": g_kva.reshape(1, -1), "w_ukv": w_ukv.astype(BF16), "sel": sel, "gk3": gk3,
        "w_uv_t": w_ukv.reshape(MLA_KV_RANK, MLA_HEADS, MLA_NOPE + MLA_V)[:, :, MLA_NOPE:]
                  .reshape(MLA_KV_RANK, MLA_HEADS * MLA_V).T.astype(BF16),
        "w_a2": w_a2_p, "b_a2": b_a2.reshape(1, -1), "g_gla": g_gla.reshape(1, -1),
        "w_out_mla": w_out_mla, "w_out_gla": w_out[MLA_HEADS * MLA_V:].astype(BF16),
        "g_norm2": g_norm2.reshape(1, -1), "w_up": w_up.astype(BF16), "w_conv": w_conv,
        "b_conv": b_conv.reshape(1, -1), "w_down": w_down.astype(BF16),
    }


def _rope_tables(n):
    inv = 1.0 / (ROPE_THETA ** (jnp.arange(HALF_ROPE, dtype=F32) / HALF_ROPE))
    ang = jnp.arange(n, dtype=F32)[:, None] * jnp.tile(inv, LANES // HALF_ROPE)[None, :]
    return jnp.cos(ang), jnp.sin(ang)


def _state_to_t(s):
    b = s.shape[0]
    return s.reshape(b, GLA_K_ALL, GLA_DV).transpose(0, 2, 1)


def _state_from_t(st):
    b = st.shape[0]
    return st.transpose(0, 2, 1).reshape(b, GLA_HEADS, GLA_DK, GLA_DV)


def _pick_tile(n, pref):
    t = min(n, pref)
    while n % t:
        t //= 2
    return t


def _layer(x, mods, w, cos_t, sin_t, cache_ckv, cache_kpe, state, hist):
    b, l, _ = x.shape
    tm = _pick_tile(l, 512)
    pos0 = 0 if cache_ckv is None else cache_ckv.shape[1]
    ckv, kpe, q, gq, gk, gv, lg, og = _inproj(x, mods, w, cos_t, sin_t, tm, pos0)
    if cache_ckv is None:
        k, vt = _keys(ckv, kpe, w, cos_t, sin_t, tm, True)
        o_mla = _attn_prompt(q, k, vt, tm, w["score_bound"])
    else:
        n_valid = pos0 + l
        lk = -(-n_valid // LANES) * LANES
        ckv_all = jnp.concatenate([cache_ckv, ckv, jnp.zeros((b, lk - n_valid, MLA_KV_RANK), F32)], axis=1)
        kpe_all = jnp.concatenate([cache_kpe, kpe, jnp.zeros((b, lk - n_valid, MLA_ROPE), F32)], axis=1)
        k, v = _keys(ckv_all, kpe_all, w, cos_t, sin_t, lk, False)
        o_mla = _attn_sample(q, k, v, n_valid)
    o_gla, st = _gla(gq, gk, gv, lg, og, _state_to_t(state), w, tm, 2 if b % 2 == 0 else 1)
    y, new_hist = _ffn(x, o_mla, o_gla, mods, hist, w, tm)
    return y, ckv, kpe, _state_from_t(st), new_hist


def kernel(x_prompt, x_sample, c_prompt, c_sample, cache_ckv, cache_kpe, state_gla, state_ffn_conv, w_ada, b_ada, g_norm1, w_in, g_qa, w_uq, g_qn, g_kva, w_ukv, g_kn, w_a2, b_a2, g_gla, w_out, g_norm2, w_up, w_conv, b_conv, w_down):
    depth = w_ada.shape[0]
    bp, lp, _ = x_prompt.shape
    bs, ls, _ = x_sample.shape
    n_pos = max(lp, cache_ckv.shape[2] + ls)
    n_pos = -(-n_pos // LANES) * LANES
    cos_t, sin_t = _rope_tables(n_pos)
    yp, ys = x_prompt, x_sample
    outs = [[] for _ in range(8)]
    for i in range(depth):
        w = _prep_weights(w_in[i], g_norm1[i], g_qa[i], w_uq[i], g_qn[i], g_kva[i], w_ukv[i], g_kn[i],
                          w_a2[i], b_a2[i], g_gla[i], w_out[i], g_norm2[i], w_up[i], w_conv[i], b_conv[i],
                          w_down[i])
        mods = _adaln(jnp.concatenate([c_prompt, c_sample], axis=0), w_ada[i], b_ada[i])
        mods = mods.reshape(bp + bs, 6, D_MODEL)
        zero_state = jnp.zeros((bp, GLA_HEADS, GLA_DK, GLA_DV), F32)
        zero_hist = jnp.zeros((bp, CONV_W - 1, FFN_DIM), F32)
        yp, a, b_, s, h = _layer(yp, mods[:bp], w, cos_t, sin_t, None, None, zero_state, zero_hist)
        for lst, val in zip(outs[:4], (a, b_, s, h)):
            lst.append(val)
        ys, a, b_, s, h = _layer(ys, mods[bp:], w, cos_t, sin_t, cache_ckv[i], cache_kpe[i],
                                 state_gla[i], state_ffn_conv[i])
        for lst, val in zip(outs[4:], (a, b_, s, h)):
            lst.append(val)
    return (yp, ys) + tuple(jnp.stack(o) for o in outs)
```

```python
import functools
import math

import jax
import jax.numpy as jnp
import numpy as np
from jax import lax
from jax.experimental import pallas as pl
from jax.experimental.pallas import tpu as pltpu

F32 = jnp.float32
BF16 = jnp.bfloat16
QK_DTYPE = jnp.bfloat16

D_MODEL = 1024
CHUNK = 64
CHUNK_SHIFT = 6
MLA_HEADS = 8
MLA_NOPE = 64
MLA_ROPE = 32
MLA_QK = MLA_NOPE + MLA_ROPE
MLA_V = 64
MLA_Q_RANK = 384
MLA_KV_RANK = 256
ROPE_THETA = 10000.0
GLA_HEADS = 4
GLA_DK = 64
GLA_DV = 128
GLA_GATE_RANK = 16
GLA_TAU = 16.0
FFN_DIM = 2816
CONV_W = 3
EPS = 1e-6

LANES = 128
HEAD_PAD = LANES
HALF_ROPE = MLA_ROPE // 2
GLA_K_ALL = GLA_HEADS * GLA_DK
GLA_V_ALL = GLA_HEADS * GLA_DV
C_QLAT = 0
C_KV = C_QLAT + MLA_Q_RANK
C_GQ = C_KV + MLA_KV_RANK
C_GK = C_GQ + GLA_K_ALL
C_GV = C_GK + GLA_K_ALL
C_OG = C_GV + GLA_V_ALL
C_MISC = C_OG + GLA_V_ALL
IN_COLS_PAD = C_MISC + LANES
FFN_TILE = 256
GLA_SEQS = 4
EXP_CLAMP = 80.0
VMEM_LIMIT = 56 * 1024 * 1024


def _const_spec(shape):
    nd = len(shape)
    return pl.BlockSpec(shape, lambda *_: (0,) * nd, pipeline_mode=pl.Buffered(1))


def _params(*sem):
    return pltpu.CompilerParams(dimension_semantics=sem, vmem_limit_bytes=VMEM_LIMIT)


def _rms_rows(x):
    return x * lax.rsqrt(jnp.mean(x * x, axis=-1, keepdims=True) + EPS)


def _ada_kernel(c_ref, w_ref, b_ref, o_ref):
    c = c_ref[...]
    s = c * jax.nn.sigmoid(c)
    o_ref[...] = jnp.dot(s, w_ref[...], precision=lax.Precision.HIGHEST,
                         preferred_element_type=F32) + b_ref[...]


def _adaln(c, w_ada, b_ada):
    n, d = c.shape
    cols = w_ada.shape[1]
    tn = 1024
    return pl.pallas_call(
        _ada_kernel,
        grid=(cols // tn,),
        in_specs=[pl.BlockSpec((n, d), lambda j: (0, 0)),
                  pl.BlockSpec((d, tn), lambda j: (0, j)),
                  pl.BlockSpec((1, tn), lambda j: (0, j))],
        out_specs=pl.BlockSpec((n, tn), lambda j: (0, j)),
        out_shape=jax.ShapeDtypeStruct((n, cols), F32),
        compiler_params=_params("arbitrary"),
        name="adaln",
    )(c, w_ada, b_ada.reshape(1, cols))


def _inproj_kernel(x_ref, mods_ref, g1_ref, win_ref, gqa_ref, wuq_ref, gq3_ref, cos_ref, sin_ref,
                   gkva_ref, wa2_ref, ba2_ref,
                   ckv_ref, kpe_ref, q_ref, gq_ref, gk_ref, gv_ref, lg_ref, og_ref, *, n_sub):
    shift = mods_ref[0, 0:1, :]
    scale = mods_ref[0, 1:2, :]
    lane = lax.broadcasted_iota(jnp.int32, (1, HEAD_PAD), 1)
    real = (lane < MLA_QK).astype(F32)
    sub = x_ref.shape[1] // n_sub

    def rows_block(rs):
        x = x_ref[0, rs, :]
        h = _rms_rows(x) * g1_ref[...] * (1.0 + scale) + shift
        p = jnp.dot(h.astype(BF16), win_ref[...], preferred_element_type=F32)

        ckv_ref[0, rs, :] = _rms_rows(p[:, C_KV:C_GQ]) * gkva_ref[...]
        misc = p[:, C_MISC:IN_COLS_PAD]
        kpe_ref[0, rs, :] = misc[:, 0:MLA_ROPE]

        qa = _rms_rows(p[:, C_QLAT:C_KV]) * gqa_ref[...]
        qu = jnp.dot(qa.astype(BF16), wuq_ref[...], preferred_element_type=F32)
        tq = gq3_ref[0:1, :] + cos_ref[rs, :] * gq3_ref[1:2, :] + sin_ref[rs, :] * gq3_ref[2:3, :]
        for hd in range(MLA_HEADS):
            xh = qu[:, hd * HEAD_PAD:(hd + 1) * HEAD_PAD]
            ss = jnp.sum(xh * xh * real, axis=-1, keepdims=True)
            r = lax.rsqrt(ss * (1.0 / MLA_QK) + EPS)
            q_ref[0, rs, hd * HEAD_PAD:(hd + 1) * HEAD_PAD] = (xh * r * tq).astype(QK_DTYPE)

        gq_ref[0, rs, :] = (p[:, C_GQ:C_GK] * (GLA_DK ** -0.5)).astype(BF16)
        gk_ref[0, rs, :] = p[:, C_GK:C_GV].astype(BF16)
        gv_ref[0, rs, :] = p[:, C_GV:C_OG].astype(BF16)
        og_ref[0, rs, :] = p[:, C_OG:C_MISC].astype(BF16)
        z = jnp.dot(misc.astype(BF16), wa2_ref[...], preferred_element_type=F32) + ba2_ref[...]
        log_sig = jnp.minimum(z, 0.0) - jnp.log1p(jnp.exp(-jnp.abs(z)))
        lg_ref[0, rs, :] = log_sig * (1.0 / GLA_TAU)

    for s in range(n_sub):
        rows_block(slice(s * sub, (s + 1) * sub))


def _inproj(x, mods, w, cos_t, sin_t, tm, pos0):
    b, l, d = x.shape
    nt = l // tm
    p0 = pos0 // tm
    row = lambda bi, li: (bi, li, 0)
    tab = lambda bi, li: (li + p0, 0)
    outs = [(MLA_KV_RANK, F32), (MLA_ROPE, F32), (MLA_HEADS * HEAD_PAD, QK_DTYPE), (GLA_K_ALL, BF16),
            (GLA_K_ALL, BF16), (GLA_V_ALL, BF16), (GLA_K_ALL, F32), (GLA_V_ALL, BF16)]
    return pl.pallas_call(
        functools.partial(_inproj_kernel, n_sub=max(1, tm // 256)),
        grid=(b, nt),
        in_specs=[pl.BlockSpec((1, tm, d), row),
                  pl.BlockSpec((1, 6, d), lambda bi, li: (bi, 0, 0)),
                  _const_spec((1, d)),
                  _const_spec((d, IN_COLS_PAD)),
                  _const_spec((1, MLA_Q_RANK)),
                  _const_spec((MLA_Q_RANK, MLA_HEADS * HEAD_PAD)),
                  _const_spec((3, HEAD_PAD)),
                  pl.BlockSpec((tm, HEAD_PAD), tab),
                  pl.BlockSpec((tm, HEAD_PAD), tab),
                  _const_spec((1, MLA_KV_RANK)),
                  _const_spec((LANES, GLA_K_ALL)),
                  _const_spec((1, GLA_K_ALL))],
        out_specs=[pl.BlockSpec((1, tm, n), row) for n, _ in outs],
        out_shape=[jax.ShapeDtypeStruct((b, l, n), dt) for n, dt in outs],
        compiler_params=_params("arbitrary", "arbitrary"),
        name="inproj",
    )(x, mods, w["g_norm1"], w["w_in"], w["g_qa"], w["w_uq"], w["gq3"], cos_t, sin_t,
      w["g_kva"], w["w_a2"], w["b_a2"])


def _keys_kernel(ckv_ref, kpe_ref, wukv_ref, wuvt_ref, sel_ref, gk3_ref, cos_ref, sin_ref, k_ref, v_ref, *,
                 transpose_v):
    ckv = ckv_ref[0].astype(BF16)
    kv = jnp.dot(ckv, wukv_ref[...], preferred_element_type=F32)
    if transpose_v:
        vt_all = lax.dot_general(wuvt_ref[...], ckv, _NT, preferred_element_type=F32)
        ones_rows = jnp.ones((MLA_V, ckv.shape[0]), BF16)
    kpe = kpe_ref[0]
    kpe_hi = kpe.astype(BF16)
    kpe_lo = (kpe - kpe_hi.astype(F32)).astype(BF16)
    uv = (jnp.dot(kpe_hi, sel_ref[...], preferred_element_type=F32)
          + jnp.dot(kpe_lo, sel_ref[...], preferred_element_type=F32))
    rot = (uv[:, :HEAD_PAD] * (cos_ref[...] * gk3_ref[1:2, :])
           + uv[:, HEAD_PAD:] * (sin_ref[...] * gk3_ref[2:3, :]))
    sp = jnp.sum(kpe * kpe, axis=-1, keepdims=True)
    lane = lax.broadcasted_iota(jnp.int32, (1, HEAD_PAD), 1)
    is_nope = lane < MLA_NOPE
    g_nope = gk3_ref[0:1, :]
    for hd in range(MLA_HEADS):
        blk = kv[:, hd * HEAD_PAD:(hd + 1) * HEAD_PAD]
        kn = jnp.where(is_nope, blk, 0.0)
        ss = jnp.sum(kn * kn, axis=-1, keepdims=True) + sp
        r = lax.rsqrt(ss * (1.0 / MLA_QK) + EPS)
        k_ref[0, :, hd * HEAD_PAD:(hd + 1) * HEAD_PAD] = ((kn * g_nope + rot) * r).astype(QK_DTYPE)
        if transpose_v:
            v_at = hd * HEAD_PAD + (hd % 2) * MLA_V
            ones_at = hd * HEAD_PAD + (1 - hd % 2) * MLA_V
            v_ref[0, ones_at:ones_at + MLA_V, :] = ones_rows
            v_ref[0, v_at:v_at + MLA_V, :] = vt_all[hd * MLA_V:(hd + 1) * MLA_V, :].astype(BF16)
        else:
            v_ref[0, :, hd * HEAD_PAD:(hd + 1) * HEAD_PAD] = jnp.where(is_nope, 1.0, blk).astype(BF16)


def _keys(ckv, kpe, w, cos_t, sin_t, tm, transpose_v):
    b, l, _ = ckv.shape
    row = lambda bi, li: (bi, li, 0)
    tab = lambda bi, li: (li, 0)
    hw = MLA_HEADS * HEAD_PAD
    if transpose_v:
        v_spec = pl.BlockSpec((1, hw, tm), lambda bi, li: (bi, 0, li))
        v_shape = jax.ShapeDtypeStruct((b, hw, l), BF16)
    else:
        v_spec = pl.BlockSpec((1, tm, hw), row)
        v_shape = jax.ShapeDtypeStruct((b, l, hw), BF16)
    return pl.pallas_call(
        functools.partial(_keys_kernel, transpose_v=transpose_v),
        grid=(b, l // tm),
        in_specs=[pl.BlockSpec((1, tm, MLA_KV_RANK), row),
                  pl.BlockSpec((1, tm, MLA_ROPE), row),
                  _const_spec((MLA_KV_RANK, hw)),
                  _const_spec((MLA_HEADS * MLA_V, MLA_KV_RANK)),
                  _const_spec((MLA_ROPE, 2 * HEAD_PAD)),
                  _const_spec((3, HEAD_PAD)),
                  pl.BlockSpec((tm, HEAD_PAD), tab),
                  pl.BlockSpec((tm, HEAD_PAD), tab)],
        out_specs=[pl.BlockSpec((1, tm, hw), row), v_spec],
        out_shape=[jax.ShapeDtypeStruct((b, l, hw), QK_DTYPE), v_shape],
        compiler_params=_params("arbitrary", "arbitrary"),
        name="keys",
    )(ckv, kpe, w["w_ukv"], w["w_uv_t"], w["sel"], w["gk3"], cos_t, sin_t)


_NT = (((1,), (1,)), ((), ()))
_TN = (((0,), (0,)), ((), ()))
MASKED = -1e30
EXP2_SAFE = 100.0


def _store_normalised(tiles, o_ref):
    odd = pl.program_id(1) % 2

    for half in range(2):
        @pl.when(odd == half)
        def _():
            lanes = slice(half * MLA_V, (half + 1) * MLA_V)
            for acc_ref, rows in tiles:
                acc = acc_ref[...]
                den = acc[(1 - half) * MLA_V:(1 - half) * MLA_V + 1, :]
                o_ref[0, rows, lanes] = (acc / den).T.astype(BF16)[:, lanes]


def _finalize_tile(acc_scr, o_ref, i, tq):
    _store_normalised([(acc_scr, pl.ds(pl.multiple_of(i * tq, tq), tq))], o_ref)


def _chunk_visible(tq):
    kc = lax.broadcasted_iota(jnp.int32, (tq, tq), 0) >> CHUNK_SHIFT
    qc = lax.broadcasted_iota(jnp.int32, (tq, tq), 1) >> CHUNK_SHIFT
    return kc <= qc


ATTN_LAG = 2
ATTN_BIG = 4
ATTN_UNROLL = 6


def _attn_bounded_kernel(iq_tab, ia_tab, j_tab, d_tab, q_ref, k_ref, vt_ref, mask_ref, o_ref, p_buf, acc_scr,
                         *, tq, nq, n_iter):
    p_buf[...] = jnp.zeros_like(p_buf)
    acc_scr[...] = jnp.zeros_like(acc_scr)
    n_slots = ATTN_LAG + 1

    def sub_step(t, slot):
        tp = jnp.maximum(t - ATTN_LAG, 0)
        jp = j_tab[tp]
        vt = vt_ref[0, :, pl.ds(pl.multiple_of(jp * tq, tq), tq)]
        cols = pl.ds(pl.multiple_of(ia_tab[tp] * tq, tq), tq)
        acc_scr[:, cols] = (jnp.where(jp == 0, 0.0, acc_scr[:, cols])
                            + jnp.dot(vt, p_buf[(slot + 1) % n_slots], preferred_element_type=F32))
        kb = k_ref[0, pl.ds(pl.multiple_of(j_tab[t] * tq, tq), tq), :]
        qt = q_ref[0, pl.ds(pl.multiple_of(iq_tab[t] * tq, tq), tq), :]
        p = jnp.exp2(lax.dot_general(kb, qt, _NT, preferred_element_type=F32)).astype(BF16)
        p_buf[slot] = p * mask_ref[d_tab[t]]

    big = ATTN_BIG * ATTN_UNROLL

    def body(it, carry):
        for u in range(big):
            sub_step(big * it + u, u % n_slots)
        return carry

    n_big, n_tail = divmod(n_iter, ATTN_BIG)
    lax.fori_loop(0, n_big, body, 0)
    for u in range(n_tail * ATTN_UNROLL):
        sub_step(jnp.int32(n_big * big + u), u % n_slots)
    _store_normalised([(acc_scr.at[:, i * tq:(i + 1) * tq], slice(i * tq, (i + 1) * tq)) for i in range(nq)],
                      o_ref)


def _attn_online_kernel(i_tab, j_tab, q_ref, k_ref, vt_ref, o_ref, s_buf, p_buf, a_buf, m_scr, acc_scr,
                        *, tq, n_steps, n_iter):
    s_buf[...] = jnp.zeros_like(s_buf)
    p_buf[...] = jnp.zeros_like(p_buf)
    a_buf[...] = jnp.zeros_like(a_buf)
    m_scr[...] = jnp.zeros_like(m_scr)
    acc_scr[...] = jnp.zeros_like(acc_scr)

    def sub_step(t, slot):
        other = 1 - slot
        t2 = jnp.maximum(t - 2, 0)
        i2, j2 = i_tab[t2], j_tab[t2]
        vt = vt_ref[0, :, pl.ds(pl.multiple_of(j2 * tq, tq), tq)]
        acc_scr[...] = a_buf[slot] * acc_scr[...] + jnp.dot(vt, p_buf[slot], preferred_element_type=F32)
        t1 = jnp.maximum(t - 1, 0)
        s = s_buf[other]
        m_old = jnp.where(j_tab[t1] == 0, -jnp.inf, m_scr[...])
        m_new = jnp.maximum(m_old, jnp.max(s, axis=0, keepdims=True))
        a_buf[other] = jnp.exp2(m_old - m_new)
        p_buf[other] = jnp.exp2(s - m_new).astype(BF16)
        m_scr[...] = m_new
        t0 = jnp.minimum(t, n_steps - 1)
        i0, j0 = i_tab[t0], j_tab[t0]
        kb = k_ref[0, pl.ds(pl.multiple_of(j0 * tq, tq), tq), :]
        qt = q_ref[0, pl.ds(pl.multiple_of(i0 * tq, tq), tq), :]
        s_buf[slot] = lax.dot_general(kb, qt, _NT, preferred_element_type=F32)

        @pl.when(i0 == j0)
        def _():
            s_buf[slot] = jnp.where(_chunk_visible(tq), s_buf[slot], MASKED)

        @pl.when(jnp.logical_and(jnp.logical_and(t >= 2, t - 2 < n_steps), i2 == j2))
        def _():
            _finalize_tile(acc_scr, o_ref, i2, tq)

    def body(it, carry):
        sub_step(2 * it, 0)
        sub_step(2 * it + 1, 1)
        return carry

    lax.fori_loop(0, n_iter, body, 0)


def _attn_specs(b, l, n_tabs, extra_in=()):
    qmap = lambda bi, hi, *_: (bi, 0, hi)
    return dict(
        num_scalar_prefetch=n_tabs,
        grid=(b, MLA_HEADS),
        in_specs=[pl.BlockSpec((1, l, HEAD_PAD), qmap),
                  pl.BlockSpec((1, l, HEAD_PAD), qmap),
                  pl.BlockSpec((1, HEAD_PAD, l), lambda bi, hi, *_: (bi, hi, 0)), *extra_in],
        out_specs=pl.BlockSpec((1, l, 2 * MLA_V), lambda bi, hi, *_: (bi, 0, hi // 2)))


def _attn_online_call(q, k, vt, tq):
    b, l, hw = q.shape
    nq = l // tq
    pairs = [(i, j) for i in range(nq) for j in range(i + 1)]
    n_steps = len(pairs)
    n_iter = (n_steps + 3) // 2
    pairs = pairs + [pairs[-1]] * (2 * n_iter - n_steps)
    i_tab = jnp.asarray([p_[0] for p_ in pairs], jnp.int32)
    j_tab = jnp.asarray([p_[1] for p_ in pairs], jnp.int32)
    grid_spec = pltpu.PrefetchScalarGridSpec(
        **_attn_specs(b, l, 2),
        scratch_shapes=[pltpu.VMEM((2, tq, tq), F32), pltpu.VMEM((2, tq, tq), BF16),
                        pltpu.VMEM((2, 1, tq), F32), pltpu.VMEM((1, tq), F32),
                        pltpu.VMEM((HEAD_PAD, tq), F32)])
    return pl.pallas_call(
        functools.partial(_attn_online_kernel, tq=tq, n_steps=n_steps, n_iter=n_iter),
        grid_spec=grid_spec,
        out_shape=jax.ShapeDtypeStruct((b, l, MLA_HEADS * MLA_V), BF16),
        compiler_params=_params("arbitrary", "arbitrary"),
        name="attn_online",
    )(i_tab, j_tab, q, k, vt)


def _attn_bounded_call(q, k, vt, tq):
    b, l, hw = q.shape
    nq = l // tq
    pairs = [(i, j) for i in range(nq) for j in range(i + 1)]
    n_iter = -(-(len(pairs) + ATTN_LAG) // ATTN_UNROLL)
    n_dummy = n_iter * ATTN_UNROLL - len(pairs)
    iq_tab = jnp.asarray([p_[0] for p_ in pairs] + [nq - 1] * n_dummy, jnp.int32)
    ia_tab = jnp.asarray([p_[0] for p_ in pairs] + [nq] * n_dummy, jnp.int32)
    j_tab = jnp.asarray([p_[1] for p_ in pairs] + [0] * n_dummy, jnp.int32)
    d_tab = jnp.asarray([int(p_[0] == p_[1]) for p_ in pairs] + [0] * n_dummy, jnp.int32)
    chunk_of = np.arange(tq) // CHUNK
    masks = jnp.asarray(np.stack([np.ones((tq, tq), np.float32),
                                  (chunk_of[:, None] <= chunk_of[None, :]).astype(np.float32)]), BF16)
    grid_spec = pltpu.PrefetchScalarGridSpec(
        **_attn_specs(b, l, 4, [pl.BlockSpec((2, tq, tq), lambda *_: (0, 0, 0), pipeline_mode=pl.Buffered(1))]),
        scratch_shapes=[pltpu.VMEM((ATTN_LAG + 1, tq, tq), BF16), pltpu.VMEM((HEAD_PAD, (nq + 1) * tq), F32)])
    return pl.pallas_call(
        functools.partial(_attn_bounded_kernel, tq=tq, nq=nq, n_iter=n_iter),
        grid_spec=grid_spec,
        out_shape=jax.ShapeDtypeStruct((b, l, MLA_HEADS * MLA_V), BF16),
        compiler_params=_params("arbitrary", "arbitrary"),
        name="attn_bounded",
    )(iq_tab, ia_tab, j_tab, d_tab, q, k, vt, masks)


def _attn_prompt(q, k, vt, tq, score_bound):
    return lax.cond(score_bound <= EXP2_SAFE,
                    functools.partial(_attn_bounded_call, tq=tq),
                    functools.partial(_attn_online_call, tq=tq),
                    q, k, vt)


def _attn_sample_kernel(q_ref, k_ref, v_ref, o_ref, *, n_valid):
    outs = []
    for hd in range(MLA_HEADS):
        cols = slice(hd * HEAD_PAD, (hd + 1) * HEAD_PAD)
        s = lax.dot_general(q_ref[0, :, cols], k_ref[0, :, cols], _NT, preferred_element_type=F32)
        col = lax.broadcasted_iota(jnp.int32, s.shape, 1)
        s = jnp.where(col < n_valid, s, MASKED)
        p = jnp.exp2(s - jnp.max(s, axis=-1, keepdims=True))
        acc = jnp.dot(p.astype(BF16), v_ref[0, :, cols], preferred_element_type=F32)
        outs.append(acc / acc[:, 0:1])
    lane = lax.broadcasted_iota(jnp.int32, (1, HEAD_PAD), 1)
    for pair in range(MLA_HEADS // 2):
        even = pltpu.roll(outs[2 * pair], MLA_V, 1)
        o_ref[0, :, pair * HEAD_PAD:(pair + 1) * HEAD_PAD] = (
            jnp.where(lane < MLA_V, even, outs[2 * pair + 1]).astype(BF16))


def _attn_sample(q, k, v, n_valid):
    b, t, hw = q.shape
    lk = k.shape[1]
    bmap = lambda bi: (bi, 0, 0)
    return pl.pallas_call(
        functools.partial(_attn_sample_kernel, n_valid=n_valid),
        grid=(b,),
        in_specs=[pl.BlockSpec((1, t, hw), bmap),
                  pl.BlockSpec((1, lk, hw), bmap),
                  pl.BlockSpec((1, lk, hw), bmap)],
        out_specs=pl.BlockSpec((1, t, MLA_HEADS * MLA_V), bmap),
        out_shape=jax.ShapeDtypeStruct((b, t, MLA_HEADS * MLA_V), BF16),
        compiler_params=_params("arbitrary"),
        name="attn_sample",
    )(q, k, v)


def _gla_kernel(gq_ref, gk_ref, gv_ref, lg_ref, og_ref, tri_ref, ggla_ref, st0_ref,
                o_ref, stf_ref, st_scr, b_scr, *, tg, nb):
    li = pl.program_id(1)

    @pl.when(li == 0)
    def _():
        st_scr[...] = st0_ref[...]

    tri = tri_ref[...]
    for bb in range(nb):
        lg = lg_ref[bb]
        lg_hi = lg.astype(BF16)
        lg_lo = (lg - lg_hi.astype(F32)).astype(BF16)
        b_scr[bb] = (jnp.dot(tri, lg_hi, preferred_element_type=F32)
                     + jnp.dot(tri, lg_lo, preferred_element_type=F32))

    lane = lax.broadcasted_iota(jnp.int32, (1, GLA_K_ALL), 1)
    head_of_lane = lane >> CHUNK_SHIFT
    ri = lax.broadcasted_iota(jnp.int32, (GLA_HEADS * CHUNK, CHUNK), 0) & (CHUNK - 1)
    cj = lax.broadcasted_iota(jnp.int32, (GLA_HEADS * CHUNK, CHUNK), 1)
    causal = cj <= ri
    g_out = ggla_ref[...]

    def chunk(bb, r0):
        b = b_scr[bb, r0:r0 + CHUNK, :]
        b_mid = b[CHUNK // 2 - 1:CHUNK // 2, :]
        b_last = b[CHUNK - 1:CHUNK, :]
        q = gq_ref[bb, r0:r0 + CHUNK, :].astype(F32)
        k = gk_ref[bb, r0:r0 + CHUNK, :].astype(F32)
        qe = q * jnp.exp(jnp.minimum(b - b_mid, EXP_CLAMP))
        ke = (k * jnp.exp(jnp.minimum(b_mid - b, EXP_CLAMP))).astype(BF16)
        qb = q * jnp.exp(b)
        kd = k * jnp.exp(b_last - b)
        zero = jnp.zeros_like(q)
        qe_st = jnp.concatenate([jnp.where(head_of_lane == hd, qe, zero) for hd in range(GLA_HEADS)],
                                axis=0).astype(BF16)
        qb_st = jnp.concatenate([jnp.where(head_of_lane == hd, qb, zero) for hd in range(GLA_HEADS)],
                                axis=0).astype(BF16)
        a_st = lax.dot_general(qe_st, ke, _NT, preferred_element_type=F32)
        a_st = jnp.where(causal, a_st, 0.0).astype(BF16)
        st = st_scr[bb]
        o_inter = lax.dot_general(qb_st, st.astype(BF16), _NT, preferred_element_type=F32)
        upd = jnp.zeros_like(st)
        for hd in range(GLA_HEADS):
            vh = gv_ref[bb, r0:r0 + CHUNK, hd * GLA_DV:(hd + 1) * GLA_DV]
            o = (o_inter[hd * CHUNK:(hd + 1) * CHUNK, :]
                 + jnp.dot(a_st[hd * CHUNK:(hd + 1) * CHUNK, :], vh, preferred_element_type=F32))
            og = og_ref[bb, r0:r0 + CHUNK, hd * GLA_DV:(hd + 1) * GLA_DV].astype(F32)
            o = _rms_rows(o) * g_out * (og * jax.nn.sigmoid(og))
            o_ref[bb, r0:r0 + CHUNK, hd * GLA_DV:(hd + 1) * GLA_DV] = o.astype(BF16)
            kd_h = jnp.where(head_of_lane == hd, kd, zero).astype(BF16)
            upd = upd + lax.dot_general(vh, kd_h, _TN, preferred_element_type=F32)
        st_scr[bb] = st * jnp.exp(b_last) + upd

    for c in range(tg // CHUNK):
        for bb in range(nb):
            chunk(bb, c * CHUNK)

    stf_ref[...] = st_scr[...]


def _gla(gq, gk, gv, lg, og, st0, w, tg, nb):
    b, l, _ = gq.shape
    row = lambda bi, li: (bi, li, 0)
    fix = lambda bi, li: (bi, 0, 0)
    pos = np.arange(tg)
    tri = jnp.asarray(((pos[:, None] // CHUNK == pos[None, :] // CHUNK) & (pos[None, :] <= pos[:, None]))
                      .astype(np.float32), BF16)
    return pl.pallas_call(
        functools.partial(_gla_kernel, tg=tg, nb=nb),
        grid=(b // nb, l // tg),
        in_specs=[pl.BlockSpec((nb, tg, GLA_K_ALL), row),
                  pl.BlockSpec((nb, tg, GLA_K_ALL), row),
                  pl.BlockSpec((nb, tg, GLA_V_ALL), row),
                  pl.BlockSpec((nb, tg, GLA_K_ALL), row),
                  pl.BlockSpec((nb, tg, GLA_V_ALL), row),
                  _const_spec((tg, tg)),
                  _const_spec((1, GLA_DV)),
                  pl.BlockSpec((nb, GLA_DV, GLA_K_ALL), fix)],
        out_specs=[pl.BlockSpec((nb, tg, GLA_V_ALL), row),
                   pl.BlockSpec((nb, GLA_DV, GLA_K_ALL), fix)],
        out_shape=[jax.ShapeDtypeStruct((b, l, GLA_V_ALL), BF16),
                   jax.ShapeDtypeStruct((b, GLA_DV, GLA_K_ALL), F32)],
        scratch_shapes=[pltpu.VMEM((nb, GLA_DV, GLA_K_ALL), F32), pltpu.VMEM((nb, tg, GLA_K_ALL), F32)],
        compiler_params=_params("arbitrary", "arbitrary"),
        name="gla",
    )(gq, gk, gv, lg, og, tri, w["g_gla"], st0)


def _ffn_kernel(x_ref, om_ref, ogl_ref, mods_ref, g2_ref, wom_ref, wog_ref, wup_ref, wcv_ref, bcv_ref,
                wdn_ref, hist_ref, y_ref, nh_ref, carry_scr, a_scr, act_scr, *, tm):
    li = pl.program_id(1)
    hist_rows = CONV_W - 1
    pad = 8

    @pl.when(li == 0)
    def _():
        carry_scr[...] = jnp.zeros_like(carry_scr)
        carry_scr[pad - hist_rows:pad, :] = hist_ref[0]

    mixed = (jnp.dot(om_ref[0], wom_ref[...], preferred_element_type=F32)
             + jnp.dot(ogl_ref[0], wog_ref[...], preferred_element_type=F32))
    x1 = x_ref[0] + mods_ref[0, 2:3, :] * mixed
    h = (_rms_rows(x1) * g2_ref[...] * (1.0 + mods_ref[0, 4:5, :]) + mods_ref[0, 3:4, :]).astype(BF16)

    for f in range(FFN_DIM // FFN_TILE):
        c0 = f * FFN_TILE
        a = jnp.dot(h, wup_ref[:, c0:c0 + FFN_TILE], preferred_element_type=F32)
        g = jnp.dot(h, wup_ref[:, FFN_DIM + c0:FFN_DIM + c0 + FFN_TILE], preferred_element_type=F32)
        a_scr[0:pad, :] = carry_scr[:, c0:c0 + FFN_TILE]
        a_scr[pad:pad + tm, :] = a
        carry_scr[:, c0:c0 + FFN_TILE] = a[tm - pad:tm, :]
        nh_ref[0, :, c0:c0 + FFN_TILE] = a[tm - hist_rows:tm, :]
        conv = (bcv_ref[:, c0:c0 + FFN_TILE]
                + wcv_ref[2:3, c0:c0 + FFN_TILE] * a
                + wcv_ref[1:2, c0:c0 + FFN_TILE] * a_scr[pad - 1:pad - 1 + tm, :]
                + wcv_ref[0:1, c0:c0 + FFN_TILE] * a_scr[pad - 2:pad - 2 + tm, :])
        act = (jax.nn.gelu(conv) * g).astype(BF16)
        act_scr[:, c0:c0 + FFN_TILE] = act

    y = jnp.dot(act_scr[...], wdn_ref[...], preferred_element_type=F32)
    y_ref[0] = x1 + mods_ref[0, 5:6, :] * y


def _ffn(x, o_mla, o_gla, mods, hist, w, tm):
    b, l, d = x.shape
    row = lambda bi, li: (bi, li, 0)
    fix = lambda bi, li: (bi, 0, 0)
    hw = MLA_HEADS * MLA_V
    return pl.pallas_call(
        functools.partial(_ffn_kernel, tm=tm),
        grid=(b, l // tm),
        in_specs=[pl.BlockSpec((1, tm, d), row),
                  pl.BlockSpec((1, tm, hw), row),
                  pl.BlockSpec((1, tm, GLA_V_ALL), row),
                  pl.BlockSpec((1, 6, d), fix),
                  _const_spec((1, d)),
                  _const_spec((hw, d)),
                  _const_spec((GLA_V_ALL, d)),
                  _const_spec((d, 2 * FFN_DIM)),
                  _const_spec((CONV_W, FFN_DIM)),
                  _const_spec((1, FFN_DIM)),
                  _const_spec((FFN_DIM, d)),
                  pl.BlockSpec((1, CONV_W - 1, FFN_DIM), fix)],
        out_specs=[pl.BlockSpec((1, tm, d), row),
                   pl.BlockSpec((1, CONV_W - 1, FFN_DIM), fix)],
        out_shape=[jax.ShapeDtypeStruct((b, l, d), F32),
                   jax.ShapeDtypeStruct((b, CONV_W - 1, FFN_DIM), F32)],
        scratch_shapes=[pltpu.VMEM((8, FFN_DIM), F32),
                        pltpu.VMEM((tm + 8, FFN_TILE), F32),
                        pltpu.VMEM((tm, FFN_DIM), BF16)],
        compiler_params=_params("arbitrary", "arbitrary"),
        name="ffn",
    )(x, o_mla, o_gla, mods, w["g_norm2"], w["w_out_mla"], w["w_out_gla"], w["w_up"], w["w_conv"],
      w["b_conv"], w["w_down"], hist)


def _prep_weights(w_in, g_norm1, g_qa, w_uq, g_qn, g_kva, w_ukv, g_kn, w_a2, b_a2, g_gla, w_out,
                  g_norm2, w_up, w_conv, b_conv, w_down):
    o, cols = 0, []
    for n in (MLA_Q_RANK, MLA_KV_RANK, MLA_ROPE, GLA_K_ALL, GLA_K_ALL, GLA_V_ALL, GLA_GATE_RANK, GLA_V_ALL):
        cols.append(w_in[:, o:o + n])
        o += n
    q_lat, kv_lat, kpe, gq, gk, gv, g_lr, og = cols
    misc_pad = jnp.zeros((D_MODEL, LANES - MLA_ROPE - GLA_GATE_RANK), w_in.dtype)
    w_in_r = jnp.concatenate([q_lat, kv_lat, gq, gk, gv, og, kpe, g_lr, misc_pad], axis=1).astype(BF16)

    wq = w_uq.reshape(MLA_Q_RANK, MLA_HEADS, MLA_QK)
    n_, r1, r2 = wq[..., :MLA_NOPE], wq[..., MLA_NOPE:MLA_NOPE + HALF_ROPE], wq[..., MLA_NOPE + HALF_ROPE:]
    w_uq_p = jnp.concatenate([n_, r1, r2, r2, r1], axis=-1).reshape(MLA_Q_RANK, MLA_HEADS * HEAD_PAD).astype(BF16)

    sc = MLA_QK ** -0.5 * math.log2(math.e)
    gn, g1, g2 = g_qn[:MLA_NOPE], g_qn[MLA_NOPE:MLA_NOPE + HALF_ROPE], g_qn[MLA_NOPE + HALF_ROPE:]
    z16, z64 = jnp.zeros((HALF_ROPE,), F32), jnp.zeros((MLA_NOPE,), F32)
    gq3 = jnp.stack([jnp.concatenate([gn, z16, z16, z16, z16]),
                     jnp.concatenate([z64, g1, g2, z16, z16]),
                     jnp.concatenate([z64, z16, z16, g2, g1])]) * math.sqrt(sc)
    kn, k1, k2 = g_kn[:MLA_NOPE], g_kn[MLA_NOPE:MLA_NOPE + HALF_ROPE], g_kn[MLA_NOPE + HALF_ROPE:]
    gk3 = jnp.stack([jnp.concatenate([kn, z16, z16, z16, z16]),
                     jnp.concatenate([z64, k1, k2, -k1, k2]),
                     jnp.concatenate([z64, -k2, k1, k2, k1])]) * math.sqrt(sc)
    eye = np.eye(HALF_ROPE, dtype=np.float32)
    zz = np.zeros((HALF_ROPE, HALF_ROPE), np.float32)
    pick1 = np.concatenate([eye, zz], axis=0)
    pick2 = np.concatenate([zz, eye], axis=0)
    z_n = np.zeros((MLA_ROPE, MLA_NOPE), np.float32)
    sel = jnp.asarray(np.concatenate([z_n, pick1, pick2, pick1, pick2, z_n, pick2, pick1, pick2, pick1], axis=1),
                      BF16)

    w_a2_p = jnp.zeros((LANES, GLA_K_ALL), F32).at[MLA_ROPE:MLA_ROPE + GLA_GATE_RANK].set(w_a2).astype(BF16)

    w_out_mla = w_out[:MLA_HEADS * MLA_V].astype(BF16)
    return {
        "score_bound": 1.01 * sc * MLA_QK * jnp.max(jnp.abs(g_qn)) * jnp.max(jnp.abs(g_kn)),
        "g_norm1": g_norm1.reshape(1, -1), "w_in": w_in_r, "g_qa": g_qa.reshape(1, -1), "w_uq": w_uq_p,
        "gq3": gq3, "g_kva": g_kva.reshape(1, -1), "w_ukv": w_ukv.astype(BF16), "sel": sel, "gk3": gk3,
        "w_uv_t": w_ukv.reshape(MLA_KV_RANK, MLA_HEADS, MLA_NOPE + MLA_V)[:, :, MLA_NOPE:]
                  .reshape(MLA_KV_RANK, MLA_HEADS * MLA_V).T.astype(BF16),
        "w_a2": w_a2_p, "b_a2": b_a2.reshape(1, -1), "g_gla": g_gla.reshape(1, -1),
        "w_out_mla": w_out_mla, "w_out_gla": w_out[MLA_HEADS * MLA_V:].astype(BF16),
        "g_norm2": g_norm2.reshape(1, -1), "w_up": w_up.astype(BF16), "w_conv": w_conv,
        "b_conv": b_conv.reshape(1, -1), "w_down": w_down.astype(BF16),
    }


def _rope_tables(n):
    inv = 1.0 / (ROPE_THETA ** (np.arange(HALF_ROPE, dtype=np.float64) / HALF_ROPE))
    ang = np.arange(n, dtype=np.float64)[:, None] * np.tile(inv, LANES // HALF_ROPE)[None, :]
    return jnp.asarray(np.cos(ang), F32), jnp.asarray(np.sin(ang), F32)


def _state_to_t(s):
    b = s.shape[0]
    return s.reshape(b, GLA_K_ALL, GLA_DV).transpose(0, 2, 1)


def _state_from_t(st):
    b = st.shape[0]
    return st.transpose(0, 2, 1).reshape(b, GLA_HEADS, GLA_DK, GLA_DV)


def _pick_tile(n, pref):
    t = min(n, pref)
    while n % t:
        t //= 2
    return t


def _layer(x, mods, w, cos_t, sin_t, cache_ckv, cache_kpe, state, hist):
    b, l, _ = x.shape
    tm = _pick_tile(l, 512)
    pos0 = 0 if cache_ckv is None else cache_ckv.shape[1]
    ckv, kpe, q, gq, gk, gv, lg, og = _inproj(x, mods, w, cos_t, sin_t, _pick_tile(l, 1024), pos0)
    if cache_ckv is None:
        k, vt = _keys(ckv, kpe, w, cos_t, sin_t, _pick_tile(l, 1024), True)
        o_mla = _attn_prompt(q, k, vt, tm, w["score_bound"])
    else:
        n_valid = pos0 + l
        lk = -(-n_valid // LANES) * LANES
        ckv_all = jnp.concatenate([cache_ckv, ckv, jnp.zeros((b, lk - n_valid, MLA_KV_RANK), F32)], axis=1)
        kpe_all = jnp.concatenate([cache_kpe, kpe, jnp.zeros((b, lk - n_valid, MLA_ROPE), F32)], axis=1)
        k, v = _keys(ckv_all, kpe_all, w, cos_t, sin_t, lk, False)
        o_mla = _attn_sample(q, k, v, n_valid)
    o_gla, st = _gla(gq, gk, gv, lg, og, _state_to_t(state), w, tm, math.gcd(b, GLA_SEQS))
    y, new_hist = _ffn(x, o_mla, o_gla, mods, hist, w, tm)
    return y, ckv, kpe, _state_from_t(st), new_hist


def kernel(x_prompt, x_sample, c_prompt, c_sample, cache_ckv, cache_kpe, state_gla, state_ffn_conv, w_ada, b_ada, g_norm1, w_in, g_qa, w_uq, g_qn, g_kva, w_ukv, g_kn, w_a2, b_a2, g_gla, w_out, g_norm2, w_up, w_conv, b_conv, w_down):
    depth = w_ada.shape[0]
    bp, lp, _ = x_prompt.shape
    bs, ls, _ = x_sample.shape
    n_pos = max(lp, cache_ckv.shape[2] + ls)
    n_pos = -(-n_pos // LANES) * LANES
    cos_t, sin_t = _rope_tables(n_pos)
    yp, ys = x_prompt, x_sample
    outs = [[] for _ in range(8)]
    for i in range(depth):
        w = _prep_weights(w_in[i], g_norm1[i], g_qa[i], w_uq[i], g_qn[i], g_kva[i], w_ukv[i], g_kn[i],
                          w_a2[i], b_a2[i], g_gla[i], w_out[i], g_norm2[i], w_up[i], w_conv[i], b_conv[i],
                          w_down[i])
        mods = _adaln(jnp.concatenate([c_prompt, c_sample], axis=0), w_ada[i], b_ada[i])
        mods = mods.reshape(bp + bs, 6, D_MODEL)
        zero_state = jnp.zeros((bp, GLA_HEADS, GLA_DK, GLA_DV), F32)
        zero_hist = jnp.zeros((bp, CONV_W - 1, FFN_DIM), F32)
        yp, a, b_, s, h = _layer(yp, mods[:bp], w, cos_t, sin_t, None, None, zero_state, zero_hist)
        for lst, val in zip(outs[:4], (a, b_, s, h)):
            lst.append(val)
        ys, a, b_, s, h = _layer(ys, mods[bp:], w, cos_t, sin_t, cache_ckv[i], cache_kpe[i],
                                 state_gla[i], state_ffn_conv[i])
        for lst, val in zip(outs[4:], (a, b_, s, h)):
            lst.append(val)
    return (yp, ys) + tuple(jnp.stack(o) for o in outs)
```

```python
import functools
import math

import jax
import jax.numpy as jnp
import numpy as np
from jax import lax
from jax.experimental import pallas as pl
from jax.experimental.pallas import tpu as pltpu

F32 = jnp.float32
BF16 = jnp.bfloat16
QK_DTYPE = jnp.bfloat16

D_MODEL = 1024
CHUNK = 64
CHUNK_SHIFT = 6
MLA_HEADS = 8
MLA_NOPE = 64
MLA_ROPE = 32
MLA_QK = MLA_NOPE + MLA_ROPE
MLA_V = 64
MLA_Q_RANK = 384
MLA_KV_RANK = 256
ROPE_THETA = 10000.0
GLA_HEADS = 4
GLA_DK = 64
GLA_DV = 128
GLA_GATE_RANK = 16
GLA_TAU = 16.0
FFN_DIM = 2816
CONV_W = 3
EPS = 1e-6

LANES = 128
HEAD_PAD = LANES
HALF_ROPE = MLA_ROPE // 2
GLA_K_ALL = GLA_HEADS * GLA_DK
GLA_V_ALL = GLA_HEADS * GLA_DV
C_QLAT = 0
C_KV = C_QLAT + MLA_Q_RANK
C_GQ = C_KV + MLA_KV_RANK
C_GK = C_GQ + GLA_K_ALL
C_GV = C_GK + GLA_K_ALL
C_OG = C_GV + GLA_V_ALL
C_MISC = C_OG + GLA_V_ALL
IN_COLS_PAD = C_MISC + LANES
FFN_TILE = 256
GLA_SEQS = 4
EXP_CLAMP = 80.0
VMEM_LIMIT = 56 * 1024 * 1024


def _const_spec(shape):
    nd = len(shape)
    return pl.BlockSpec(shape, lambda *_: (0,) * nd, pipeline_mode=pl.Buffered(1))


def _params(*sem):
    return pltpu.CompilerParams(dimension_semantics=sem, vmem_limit_bytes=VMEM_LIMIT)


def _rms_rows(x):
    return x * lax.rsqrt(jnp.mean(x * x, axis=-1, keepdims=True) + EPS)


def _ada_kernel(c_ref, w_ref, b_ref, o_ref):
    c = c_ref[...]
    s = c * jax.nn.sigmoid(c)
    o_ref[...] = jnp.dot(s, w_ref[...], precision=lax.Precision.HIGHEST,
                         preferred_element_type=F32) + b_ref[...]


def _adaln(c, w_ada, b_ada):
    n, d = c.shape
    cols = w_ada.shape[1]
    tn = 1024
    return pl.pallas_call(
        _ada_kernel,
        grid=(cols // tn,),
        in_specs=[pl.BlockSpec((n, d), lambda j: (0, 0)),
                  pl.BlockSpec((d, tn), lambda j: (0, j)),
                  pl.BlockSpec((1, tn), lambda j: (0, j))],
        out_specs=pl.BlockSpec((n, tn), lambda j: (0, j)),
        out_shape=jax.ShapeDtypeStruct((n, cols), F32),
        compiler_params=_params("arbitrary"),
        name="adaln",
    )(c, w_ada, b_ada.reshape(1, cols))


def _inproj_kernel(x_ref, mods_ref, g1_ref, win_ref, gqa_ref, wuq_ref, gq3_ref, cos_ref, sin_ref,
                   gkva_ref, wa2_ref, ba2_ref,
                   ckv_ref, kpe_ref, q_ref, gq_ref, gk_ref, gv_ref, lg_ref, og_ref, *, n_sub):
    shift = mods_ref[0, 0:1, :]
    scale = mods_ref[0, 1:2, :]
    lane = lax.broadcasted_iota(jnp.int32, (1, HEAD_PAD), 1)
    real = (lane < MLA_QK).astype(F32)
    sub = x_ref.shape[1] // n_sub

    def rows_block(rs):
        x = x_ref[0, rs, :]
        h = _rms_rows(x) * g1_ref[...] * (1.0 + scale) + shift
        p = jnp.dot(h.astype(BF16), win_ref[...], preferred_element_type=F32)

        ckv_ref[0, rs, :] = _rms_rows(p[:, C_KV:C_GQ]) * gkva_ref[...]
        misc = p[:, C_MISC:IN_COLS_PAD]
        kpe_ref[0, rs, :] = misc[:, 0:MLA_ROPE]

        qa = _rms_rows(p[:, C_QLAT:C_KV]) * gqa_ref[...]
        qu = jnp.dot(qa.astype(BF16), wuq_ref[...], preferred_element_type=F32)
        tq = gq3_ref[0:1, :] + cos_ref[rs, :] * gq3_ref[1:2, :] + sin_ref[rs, :] * gq3_ref[2:3, :]
        for hd in range(MLA_HEADS):
            xh = qu[:, hd * HEAD_PAD:(hd + 1) * HEAD_PAD]
            ss = jnp.sum(xh * xh * real, axis=-1, keepdims=True)
            r = lax.rsqrt(ss * (1.0 / MLA_QK) + EPS)
            q_ref[0, rs, hd * HEAD_PAD:(hd + 1) * HEAD_PAD] = (xh * r * tq).astype(QK_DTYPE)

        gq_ref[0, rs, :] = (p[:, C_GQ:C_GK] * (GLA_DK ** -0.5)).astype(BF16)
        gk_ref[0, rs, :] = p[:, C_GK:C_GV].astype(BF16)
        gv_ref[0, rs, :] = p[:, C_GV:C_OG].astype(BF16)
        og_ref[0, rs, :] = p[:, C_OG:C_MISC].astype(BF16)
        z = jnp.dot(misc.astype(BF16), wa2_ref[...], preferred_element_type=F32) + ba2_ref[...]
        log_sig = jnp.minimum(z, 0.0) - jnp.log1p(jnp.exp(-jnp.abs(z)))
        lg_ref[0, rs, :] = log_sig * (1.0 / GLA_TAU)

    for s in range(n_sub):
        rows_block(slice(s * sub, (s + 1) * sub))


def _inproj(x, mods, w, cos_t, sin_t, tm, pos0):
    b, l, d = x.shape
    nt = l // tm
    p0 = pos0 // tm
    row = lambda bi, li: (bi, li, 0)
    tab = lambda bi, li: (li + p0, 0)
    outs = [(MLA_KV_RANK, F32), (MLA_ROPE, F32), (MLA_HEADS * HEAD_PAD, QK_DTYPE), (GLA_K_ALL, BF16),
            (GLA_K_ALL, BF16), (GLA_V_ALL, BF16), (GLA_K_ALL, F32), (GLA_V_ALL, BF16)]
    return pl.pallas_call(
        functools.partial(_inproj_kernel, n_sub=max(1, tm // 256)),
        grid=(b, nt),
        in_specs=[pl.BlockSpec((1, tm, d), row),
                  pl.BlockSpec((1, 6, d), lambda bi, li: (bi, 0, 0)),
                  _const_spec((1, d)),
                  _const_spec((d, IN_COLS_PAD)),
                  _const_spec((1, MLA_Q_RANK)),
                  _const_spec((MLA_Q_RANK, MLA_HEADS * HEAD_PAD)),
                  _const_spec((3, HEAD_PAD)),
                  pl.BlockSpec((tm, HEAD_PAD), tab),
                  pl.BlockSpec((tm, HEAD_PAD), tab),
                  _const_spec((1, MLA_KV_RANK)),
                  _const_spec((LANES, GLA_K_ALL)),
                  _const_spec((1, GLA_K_ALL))],
        out_specs=[pl.BlockSpec((1, tm, n), row) for n, _ in outs],
        out_shape=[jax.ShapeDtypeStruct((b, l, n), dt) for n, dt in outs],
        compiler_params=_params("arbitrary", "arbitrary"),
        name="inproj",
    )(x, mods, w["g_norm1"], w["w_in"], w["g_qa"], w["w_uq"], w["gq3"], cos_t, sin_t,
      w["g_kva"], w["w_a2"], w["b_a2"])


def _keys_kernel(ckv_ref, kpe_ref, wukv_ref, wuvt_ref, sel_ref, gk3_ref, cos_ref, sin_ref, k_ref, v_ref, *,
                 transpose_v):
    ckv = ckv_ref[0].astype(BF16)
    kv = jnp.dot(ckv, wukv_ref[...], preferred_element_type=F32)
    if transpose_v:
        vt_all = lax.dot_general(wuvt_ref[...], ckv, _NT, preferred_element_type=F32)
        ones_rows = jnp.ones((DEN_ROWS, ckv.shape[0]), BF16)
        zero_rows = jnp.zeros((MLA_V - DEN_ROWS, ckv.shape[0]), BF16)
    kpe = kpe_ref[0]
    kpe_hi = kpe.astype(BF16)
    kpe_lo = (kpe - kpe_hi.astype(F32)).astype(BF16)
    uv = (jnp.dot(kpe_hi, sel_ref[...], preferred_element_type=F32)
          + jnp.dot(kpe_lo, sel_ref[...], preferred_element_type=F32))
    rot = (uv[:, :HEAD_PAD] * (cos_ref[...] * gk3_ref[1:2, :])
           + uv[:, HEAD_PAD:] * (sin_ref[...] * gk3_ref[2:3, :]))
    sp = jnp.sum(kpe * kpe, axis=-1, keepdims=True)
    lane = lax.broadcasted_iota(jnp.int32, (1, HEAD_PAD), 1)
    is_nope = lane < MLA_NOPE
    g_nope = gk3_ref[0:1, :]
    for hd in range(MLA_HEADS):
        blk = kv[:, hd * HEAD_PAD:(hd + 1) * HEAD_PAD]
        kn = jnp.where(is_nope, blk, 0.0)
        ss = jnp.sum(kn * kn, axis=-1, keepdims=True) + sp
        r = lax.rsqrt(ss * (1.0 / MLA_QK) + EPS)
        k_ref[0, :, hd * HEAD_PAD:(hd + 1) * HEAD_PAD] = ((kn * g_nope + rot) * r).astype(QK_DTYPE)
        if transpose_v:
            v_at = hd * HEAD_PAD + (hd % 2) * MLA_V
            ones_at = hd * HEAD_PAD + _den_row(hd % 2)
            zero_at = hd * HEAD_PAD + (MLA_V + DEN_ROWS if hd % 2 == 0 else 0)
            v_ref[0, ones_at:ones_at + DEN_ROWS, :] = ones_rows
            v_ref[0, zero_at:zero_at + MLA_V - DEN_ROWS, :] = zero_rows
            v_ref[0, v_at:v_at + MLA_V, :] = vt_all[hd * MLA_V:(hd + 1) * MLA_V, :].astype(BF16)
        else:
            v_ref[0, :, hd * HEAD_PAD:(hd + 1) * HEAD_PAD] = jnp.where(is_nope, 1.0, blk).astype(BF16)


def _keys(ckv, kpe, w, cos_t, sin_t, tm, transpose_v):
    b, l, _ = ckv.shape
    row = lambda bi, li: (bi, li, 0)
    tab = lambda bi, li: (li, 0)
    hw = MLA_HEADS * HEAD_PAD
    if transpose_v:
        v_spec = pl.BlockSpec((1, hw, tm), lambda bi, li: (bi, 0, li))
        v_shape = jax.ShapeDtypeStruct((b, hw, l), BF16)
    else:
        v_spec = pl.BlockSpec((1, tm, hw), row)
        v_shape = jax.ShapeDtypeStruct((b, l, hw), BF16)
    return pl.pallas_call(
        functools.partial(_keys_kernel, transpose_v=transpose_v),
        grid=(b, l // tm),
        in_specs=[pl.BlockSpec((1, tm, MLA_KV_RANK), row),
                  pl.BlockSpec((1, tm, MLA_ROPE), row),
                  _const_spec((MLA_KV_RANK, hw)),
                  _const_spec((MLA_HEADS * MLA_V, MLA_KV_RANK)),
                  _const_spec((MLA_ROPE, 2 * HEAD_PAD)),
                  _const_spec((3, HEAD_PAD)),
                  pl.BlockSpec((tm, HEAD_PAD), tab),
                  pl.BlockSpec((tm, HEAD_PAD), tab)],
        out_specs=[pl.BlockSpec((1, tm, hw), row), v_spec],
        out_shape=[jax.ShapeDtypeStruct((b, l, hw), QK_DTYPE), v_shape],
        compiler_params=_params("arbitrary", "arbitrary"),
        name="keys",
    )(ckv, kpe, w["w_ukv"], w["w_uv_t"], w["sel"], w["gk3"], cos_t, sin_t)


_NT = (((1,), (1,)), ((), ()))
_TN = (((0,), (0,)), ((), ()))
MASKED = -1e30
EXP2_SAFE = 100.0


DEN_ROWS = 16
PV_ROWS = MLA_V + DEN_ROWS


def _den_row(parity):
    return MLA_V if parity == 0 else MLA_V - DEN_ROWS


def _pv_window():
    start = (pl.program_id(1) % 2) * (MLA_V - DEN_ROWS)
    return pl.ds(pl.multiple_of(start, DEN_ROWS), PV_ROWS)


def _store_normalised(tiles, o_ref):
    odd = pl.program_id(1) % 2

    for half in range(2):
        @pl.when(odd == half)
        def _():
            lanes = slice(half * MLA_V, (half + 1) * MLA_V)
            for acc_ref, rows in tiles:
                acc = acc_ref[...]
                den = acc[_den_row(half):_den_row(half) + 1, :]
                o_ref[0, rows, lanes] = (acc / den).T.astype(BF16)[:, lanes]


def _finalize_tile(acc_scr, o_ref, i, tq):
    _store_normalised([(acc_scr, pl.ds(pl.multiple_of(i * tq, tq), tq))], o_ref)


def _chunk_visible(tq):
    kc = lax.broadcasted_iota(jnp.int32, (tq, tq), 0) >> CHUNK_SHIFT
    qc = lax.broadcasted_iota(jnp.int32, (tq, tq), 1) >> CHUNK_SHIFT
    return kc <= qc


ATTN_LAG = 2
ATTN_BIG = 4
ATTN_UNROLL = 6


def _attn_bounded_kernel(iq_tab, ia_tab, j_tab, d_tab, q_ref, k_ref, vt_ref, mask_ref, o_ref, p_buf, acc_scr,
                         *, tq, nq, n_iter):
    p_buf[...] = jnp.zeros_like(p_buf)
    acc_scr[...] = jnp.zeros_like(acc_scr)
    n_slots = ATTN_LAG + 1
    win = _pv_window()

    def sub_step(t, slot):
        tp = jnp.maximum(t - ATTN_LAG, 0)
        jp = j_tab[tp]
        vt = vt_ref[0, win, pl.ds(pl.multiple_of(jp * tq, tq), tq)]
        cols = pl.ds(pl.multiple_of(ia_tab[tp] * tq, tq), tq)
        acc_scr[win, cols] = (jnp.where(jp == 0, 0.0, acc_scr[win, cols])
                              + jnp.dot(vt, p_buf[(slot + 1) % n_slots], preferred_element_type=F32))
        kb = k_ref[0, pl.ds(pl.multiple_of(j_tab[t] * tq, tq), tq), :]
        qt = q_ref[0, pl.ds(pl.multiple_of(iq_tab[t] * tq, tq), tq), :]
        p = jnp.exp2(lax.dot_general(kb, qt, _NT, preferred_element_type=F32)).astype(BF16)
        p_buf[slot] = p * mask_ref[d_tab[t]]

    big = ATTN_BIG * ATTN_UNROLL

    def body(it, carry):
        for u in range(big):
            sub_step(big * it + u, u % n_slots)
        return carry

    n_big, n_tail = divmod(n_iter, ATTN_BIG)
    lax.fori_loop(0, n_big, body, 0)
    for u in range(n_tail * ATTN_UNROLL):
        sub_step(jnp.int32(n_big * big + u), u % n_slots)
    _store_normalised([(acc_scr.at[:, i * tq:(i + 1) * tq], slice(i * tq, (i + 1) * tq)) for i in range(nq)],
                      o_ref)


def _attn_online_kernel(i_tab, j_tab, q_ref, k_ref, vt_ref, o_ref, s_buf, p_buf, a_buf, m_scr, acc_scr,
                        *, tq, n_steps, n_iter):
    s_buf[...] = jnp.zeros_like(s_buf)
    p_buf[...] = jnp.zeros_like(p_buf)
    a_buf[...] = jnp.zeros_like(a_buf)
    m_scr[...] = jnp.zeros_like(m_scr)
    acc_scr[...] = jnp.zeros_like(acc_scr)

    def sub_step(t, slot):
        other = 1 - slot
        t2 = jnp.maximum(t - 2, 0)
        i2, j2 = i_tab[t2], j_tab[t2]
        vt = vt_ref[0, :, pl.ds(pl.multiple_of(j2 * tq, tq), tq)]
        acc_scr[...] = a_buf[slot] * acc_scr[...] + jnp.dot(vt, p_buf[slot], preferred_element_type=F32)
        t1 = jnp.maximum(t - 1, 0)
        s = s_buf[other]
        m_old = jnp.where(j_tab[t1] == 0, -jnp.inf, m_scr[...])
        m_new = jnp.maximum(m_old, jnp.max(s, axis=0, keepdims=True))
        a_buf[other] = jnp.exp2(m_old - m_new)
        p_buf[other] = jnp.exp2(s - m_new).astype(BF16)
        m_scr[...] = m_new
        t0 = jnp.minimum(t, n_steps - 1)
        i0, j0 = i_tab[t0], j_tab[t0]
        kb = k_ref[0, pl.ds(pl.multiple_of(j0 * tq, tq), tq), :]
        qt = q_ref[0, pl.ds(pl.multiple_of(i0 * tq, tq), tq), :]
        s_buf[slot] = lax.dot_general(kb, qt, _NT, preferred_element_type=F32)

        @pl.when(i0 == j0)
        def _():
            s_buf[slot] = jnp.where(_chunk_visible(tq), s_buf[slot], MASKED)

        @pl.when(jnp.logical_and(jnp.logical_and(t >= 2, t - 2 < n_steps), i2 == j2))
        def _():
            _finalize_tile(acc_scr, o_ref, i2, tq)

    def body(it, carry):
        sub_step(2 * it, 0)
        sub_step(2 * it + 1, 1)
        return carry

    lax.fori_loop(0, n_iter, body, 0)


def _attn_specs(b, l, n_tabs, extra_in=()):
    qmap = lambda bi, hi, *_: (bi, 0, hi)
    return dict(
        num_scalar_prefetch=n_tabs,
        grid=(b, MLA_HEADS),
        in_specs=[pl.BlockSpec((1, l, HEAD_PAD), qmap),
                  pl.BlockSpec((1, l, HEAD_PAD), qmap),
                  pl.BlockSpec((1, HEAD_PAD, l), lambda bi, hi, *_: (bi, hi, 0)), *extra_in],
        out_specs=pl.BlockSpec((1, l, 2 * MLA_V), lambda bi, hi, *_: (bi, 0, hi // 2)))


def _attn_online_call(q, k, vt, tq):
    b, l, hw = q.shape
    nq = l // tq
    pairs = [(i, j) for i in range(nq) for j in range(i + 1)]
    n_steps = len(pairs)
    n_iter = (n_steps + 3) // 2
    pairs = pairs + [pairs[-1]] * (2 * n_iter - n_steps)
    i_tab = jnp.asarray([p_[0] for p_ in pairs], jnp.int32)
    j_tab = jnp.asarray([p_[1] for p_ in pairs], jnp.int32)
    grid_spec = pltpu.PrefetchScalarGridSpec(
        **_attn_specs(b, l, 2),
        scratch_shapes=[pltpu.VMEM((2, tq, tq), F32), pltpu.VMEM((2, tq, tq), BF16),
                        pltpu.VMEM((2, 1, tq), F32), pltpu.VMEM((1, tq), F32),
                        pltpu.VMEM((HEAD_PAD, tq), F32)])
    return pl.pallas_call(
        functools.partial(_attn_online_kernel, tq=tq, n_steps=n_steps, n_iter=n_iter),
        grid_spec=grid_spec,
        out_shape=jax.ShapeDtypeStruct((b, l, MLA_HEADS * MLA_V), BF16),
        compiler_params=_params("arbitrary", "arbitrary"),
        name="attn_online",
    )(i_tab, j_tab, q, k, vt)


def _attn_bounded_call(q, k, vt, tq):
    b, l, hw = q.shape
    nq = l // tq
    pairs = [(i, j) for i in range(nq) for j in range(i + 1)]
    n_iter = -(-(len(pairs) + ATTN_LAG) // ATTN_UNROLL)
    n_dummy = n_iter * ATTN_UNROLL - len(pairs)
    iq_tab = jnp.asarray([p_[0] for p_ in pairs] + [nq - 1] * n_dummy, jnp.int32)
    ia_tab = jnp.asarray([p_[0] for p_ in pairs] + [nq] * n_dummy, jnp.int32)
    j_tab = jnp.asarray([p_[1] for p_ in pairs] + [0] * n_dummy, jnp.int32)
    d_tab = jnp.asarray([int(p_[0] == p_[1]) for p_ in pairs] + [0] * n_dummy, jnp.int32)
    chunk_of = np.arange(tq) // CHUNK
    masks = jnp.asarray(np.stack([np.ones((tq, tq), np.float32),
                                  (chunk_of[:, None] <= chunk_of[None, :]).astype(np.float32)]), BF16)
    grid_spec = pltpu.PrefetchScalarGridSpec(
        **_attn_specs(b, l, 4, [pl.BlockSpec((2, tq, tq), lambda *_: (0, 0, 0), pipeline_mode=pl.Buffered(1))]),
        scratch_shapes=[pltpu.VMEM((ATTN_LAG + 1, tq, tq), BF16), pltpu.VMEM((HEAD_PAD, (nq + 1) * tq), F32)])
    return pl.pallas_call(
        functools.partial(_attn_bounded_kernel, tq=tq, nq=nq, n_iter=n_iter),
        grid_spec=grid_spec,
        out_shape=jax.ShapeDtypeStruct((b, l, MLA_HEADS * MLA_V), BF16),
        compiler_params=_params("arbitrary", "arbitrary"),
        name="attn_bounded",
    )(iq_tab, ia_tab, j_tab, d_tab, q, k, vt, masks)


def _attn_prompt(q, k, vt, tq, score_bound):
    return lax.cond(score_bound <= EXP2_SAFE,
                    functools.partial(_attn_bounded_call, tq=tq),
                    functools.partial(_attn_online_call, tq=tq),
                    q, k, vt)


def _attn_sample_kernel(q_ref, k_ref, v_ref, o_ref, *, n_valid):
    outs = []
    for hd in range(MLA_HEADS):
        cols = slice(hd * HEAD_PAD, (hd + 1) * HEAD_PAD)
        s = lax.dot_general(q_ref[0, :, cols], k_ref[0, :, cols], _NT, preferred_element_type=F32)
        col = lax.broadcasted_iota(jnp.int32, s.shape, 1)
        s = jnp.where(col < n_valid, s, MASKED)
        p = jnp.exp2(s - jnp.max(s, axis=-1, keepdims=True))
        acc = jnp.dot(p.astype(BF16), v_ref[0, :, cols], preferred_element_type=F32)
        outs.append(acc / acc[:, 0:1])
    lane = lax.broadcasted_iota(jnp.int32, (1, HEAD_PAD), 1)
    for pair in range(MLA_HEADS // 2):
        even = pltpu.roll(outs[2 * pair], MLA_V, 1)
        o_ref[0, :, pair * HEAD_PAD:(pair + 1) * HEAD_PAD] = (
            jnp.where(lane < MLA_V, even, outs[2 * pair + 1]).astype(BF16))


def _attn_sample(q, k, v, n_valid):
    b, t, hw = q.shape
    lk = k.shape[1]
    bmap = lambda bi: (bi, 0, 0)
    return pl.pallas_call(
        functools.partial(_attn_sample_kernel, n_valid=n_valid),
        grid=(b,),
        in_specs=[pl.BlockSpec((1, t, hw), bmap),
                  pl.BlockSpec((1, lk, hw), bmap),
                  pl.BlockSpec((1, lk, hw), bmap)],
        out_specs=pl.BlockSpec((1, t, MLA_HEADS * MLA_V), bmap),
        out_shape=jax.ShapeDtypeStruct((b, t, MLA_HEADS * MLA_V), BF16),
        compiler_params=_params("arbitrary"),
        name="attn_sample",
    )(q, k, v)


def _gla_kernel(gq_ref, gk_ref, gv_ref, lg_ref, og_ref, tri_ref, ggla_ref, st0_ref,
                o_ref, stf_ref, st_scr, b_scr, *, tg, nb):
    li = pl.program_id(1)

    @pl.when(li == 0)
    def _():
        st_scr[...] = st0_ref[...]

    tri = tri_ref[...]
    tb = tri.shape[0]
    for bb in range(nb):
        for r0 in range(0, tg, tb):
            lg = lg_ref[bb, r0:r0 + tb, :]
            lg_hi = lg.astype(BF16)
            lg_lo = (lg - lg_hi.astype(F32)).astype(BF16)
            b_scr[bb, r0:r0 + tb, :] = (jnp.dot(tri, lg_hi, preferred_element_type=F32)
                                        + jnp.dot(tri, lg_lo, preferred_element_type=F32))

    lane = lax.broadcasted_iota(jnp.int32, (1, GLA_K_ALL), 1)
    head_of_lane = lane >> CHUNK_SHIFT
    ri = lax.broadcasted_iota(jnp.int32, (GLA_HEADS * CHUNK, CHUNK), 0) & (CHUNK - 1)
    cj = lax.broadcasted_iota(jnp.int32, (GLA_HEADS * CHUNK, CHUNK), 1)
    causal = cj <= ri
    g_out = ggla_ref[...]

    def chunk(bb, r0):
        b = b_scr[bb, r0:r0 + CHUNK, :]
        b_mid = b[CHUNK // 2 - 1:CHUNK // 2, :]
        b_last = b[CHUNK - 1:CHUNK, :]
        q = gq_ref[bb, r0:r0 + CHUNK, :].astype(F32)
        k = gk_ref[bb, r0:r0 + CHUNK, :].astype(F32)
        qe = q * jnp.exp(jnp.minimum(b - b_mid, EXP_CLAMP))
        ke = (k * jnp.exp(jnp.minimum(b_mid - b, EXP_CLAMP))).astype(BF16)
        qb = q * jnp.exp(b)
        kd = k * jnp.exp(b_last - b)
        qe, qb = qe.astype(BF16), qb.astype(BF16)
        zero = jnp.zeros_like(qe)
        qe_st = jnp.concatenate([jnp.where(head_of_lane == hd, qe, zero) for hd in range(GLA_HEADS)], axis=0)
        qb_st = jnp.concatenate([jnp.where(head_of_lane == hd, qb, zero) for hd in range(GLA_HEADS)], axis=0)
        a_st = lax.dot_general(qe_st, ke, _NT, preferred_element_type=F32)
        a_st = jnp.where(causal, a_st, 0.0).astype(BF16)
        st = st_scr[bb]
        o_inter = lax.dot_general(qb_st, st.astype(BF16), _NT, preferred_element_type=F32)
        vk = lax.dot_general(gv_ref[bb, r0:r0 + CHUNK, :], kd.astype(BF16), _TN,
                             preferred_element_type=F32)
        upd = vk[0:GLA_DV, :]
        for hd in range(1, GLA_HEADS):
            upd = jnp.where(head_of_lane == hd, vk[hd * GLA_DV:(hd + 1) * GLA_DV, :], upd)
        for hd in range(GLA_HEADS):
            vh = gv_ref[bb, r0:r0 + CHUNK, hd * GLA_DV:(hd + 1) * GLA_DV]
            o = (o_inter[hd * CHUNK:(hd + 1) * CHUNK, :]
                 + jnp.dot(a_st[hd * CHUNK:(hd + 1) * CHUNK, :], vh, preferred_element_type=F32))
            og = og_ref[bb, r0:r0 + CHUNK, hd * GLA_DV:(hd + 1) * GLA_DV].astype(F32)
            o = _rms_rows(o) * g_out * (og * jax.nn.sigmoid(og))
            o_ref[bb, r0:r0 + CHUNK, hd * GLA_DV:(hd + 1) * GLA_DV] = o.astype(BF16)
        st_scr[bb] = st * jnp.exp(b_last) + upd

    for c in range(tg // CHUNK):
        for bb in range(nb):
            chunk(bb, c * CHUNK)

    stf_ref[...] = st_scr[...]


def _gla(gq, gk, gv, lg, og, st0, w, tg, nb):
    b, l, _ = gq.shape
    row = lambda bi, li: (bi, li, 0)
    fix = lambda bi, li: (bi, 0, 0)
    tb = min(tg, 256)
    pos = np.arange(tb)
    tri = jnp.asarray(((pos[:, None] // CHUNK == pos[None, :] // CHUNK) & (pos[None, :] <= pos[:, None]))
                      .astype(np.float32), BF16)
    return pl.pallas_call(
        functools.partial(_gla_kernel, tg=tg, nb=nb),
        grid=(b // nb, l // tg),
        in_specs=[pl.BlockSpec((nb, tg, GLA_K_ALL), row),
                  pl.BlockSpec((nb, tg, GLA_K_ALL), row),
                  pl.BlockSpec((nb, tg, GLA_V_ALL), row),
                  pl.BlockSpec((nb, tg, GLA_K_ALL), row),
                  pl.BlockSpec((nb, tg, GLA_V_ALL), row),
                  _const_spec((tb, tb)),
                  _const_spec((1, GLA_DV)),
                  pl.BlockSpec((nb, GLA_DV, GLA_K_ALL), fix)],
        out_specs=[pl.BlockSpec((nb, tg, GLA_V_ALL), row),
                   pl.BlockSpec((nb, GLA_DV, GLA_K_ALL), fix)],
        out_shape=[jax.ShapeDtypeStruct((b, l, GLA_V_ALL), BF16),
                   jax.ShapeDtypeStruct((b, GLA_DV, GLA_K_ALL), F32)],
        scratch_shapes=[pltpu.VMEM((nb, GLA_DV, GLA_K_ALL), F32), pltpu.VMEM((nb, tg, GLA_K_ALL), F32)],
        compiler_params=_params("arbitrary", "arbitrary"),
        name="gla",
    )(gq, gk, gv, lg, og, tri, w["g_gla"], st0)


def _ffn_kernel(x_ref, om_ref, ogl_ref, mods_ref, g2_ref, wom_ref, wog_ref, wup_ref, wcv_ref, bcv_ref,
                wdn_ref, hist_ref, y_ref, nh_ref, carry_scr, a_scr, act_scr, *, tm):
    li = pl.program_id(1)
    hist_rows = CONV_W - 1
    pad = 8

    @pl.when(li == 0)
    def _():
        carry_scr[...] = jnp.zeros_like(carry_scr)
        carry_scr[pad - hist_rows:pad, :] = hist_ref[0]

    mixed = (jnp.dot(om_ref[0], wom_ref[...], preferred_element_type=F32)
             + jnp.dot(ogl_ref[0], wog_ref[...], preferred_element_type=F32))
    x1 = x_ref[0] + mods_ref[0, 2:3, :] * mixed
    h = (_rms_rows(x1) * g2_ref[...] * (1.0 + mods_ref[0, 4:5, :]) + mods_ref[0, 3:4, :]).astype(BF16)

    for f in range(FFN_DIM // FFN_TILE):
        c0 = f * FFN_TILE
        a = jnp.dot(h, wup_ref[:, c0:c0 + FFN_TILE], preferred_element_type=F32)
        g = jnp.dot(h, wup_ref[:, FFN_DIM + c0:FFN_DIM + c0 + FFN_TILE], preferred_element_type=F32)
        a_scr[0:pad, :] = carry_scr[:, c0:c0 + FFN_TILE]
        a_scr[pad:pad + tm, :] = a
        carry_scr[:, c0:c0 + FFN_TILE] = a[tm - pad:tm, :]
        nh_ref[0, :, c0:c0 + FFN_TILE] = a[tm - hist_rows:tm, :]
        conv = (bcv_ref[:, c0:c0 + FFN_TILE]
                + wcv_ref[2:3, c0:c0 + FFN_TILE] * a
                + wcv_ref[1:2, c0:c0 + FFN_TILE] * a_scr[pad - 1:pad - 1 + tm, :]
                + wcv_ref[0:1, c0:c0 + FFN_TILE] * a_scr[pad - 2:pad - 2 + tm, :])
        act = (jax.nn.gelu(conv) * g).astype(BF16)
        act_scr[:, c0:c0 + FFN_TILE] = act

    y = jnp.dot(act_scr[...], wdn_ref[...], preferred_element_type=F32)
    y_ref[0] = x1 + mods_ref[0, 5:6, :] * y


def _ffn(x, o_mla, o_gla, mods, hist, w, tm):
    b, l, d = x.shape
    row = lambda bi, li: (bi, li, 0)
    fix = lambda bi, li: (bi, 0, 0)
    hw = MLA_HEADS * MLA_V
    return pl.pallas_call(
        functools.partial(_ffn_kernel, tm=tm),
        grid=(b, l // tm),
        in_specs=[pl.BlockSpec((1, tm, d), row),
                  pl.BlockSpec((1, tm, hw), row),
                  pl.BlockSpec((1, tm, GLA_V_ALL), row),
                  pl.BlockSpec((1, 6, d), fix),
                  _const_spec((1, d)),
                  _const_spec((hw, d)),
                  _const_spec((GLA_V_ALL, d)),
                  _const_spec((d, 2 * FFN_DIM)),
                  _const_spec((CONV_W, FFN_DIM)),
                  _const_spec((1, FFN_DIM)),
                  _const_spec((FFN_DIM, d)),
                  pl.BlockSpec((1, CONV_W - 1, FFN_DIM), fix)],
        out_specs=[pl.BlockSpec((1, tm, d), row),
                   pl.BlockSpec((1, CONV_W - 1, FFN_DIM), fix)],
        out_shape=[jax.ShapeDtypeStruct((b, l, d), F32),
                   jax.ShapeDtypeStruct((b, CONV_W - 1, FFN_DIM), F32)],
        scratch_shapes=[pltpu.VMEM((8, FFN_DIM), F32),
                        pltpu.VMEM((tm + 8, FFN_TILE), F32),
                        pltpu.VMEM((tm, FFN_DIM), BF16)],
        compiler_params=_params("arbitrary", "arbitrary"),
        name="ffn",
    )(x, o_mla, o_gla, mods, w["g_norm2"], w["w_out_mla"], w["w_out_gla"], w["w_up"], w["w_conv"],
      w["b_conv"], w["w_down"], hist)


def _prep_weights(w_in, g_norm1, g_qa, w_uq, g_qn, g_kva, w_ukv, g_kn, w_a2, b_a2, g_gla, w_out,
                  g_norm2, w_up, w_conv, b_conv, w_down):
    o, cols = 0, []
    for n in (MLA_Q_RANK, MLA_KV_RANK, MLA_ROPE, GLA_K_ALL, GLA_K_ALL, GLA_V_ALL, GLA_GATE_RANK, GLA_V_ALL):
        cols.append(w_in[:, o:o + n])
        o += n
    q_lat, kv_lat, kpe, gq, gk, gv, g_lr, og = cols
    misc_pad = jnp.zeros((D_MODEL, LANES - MLA_ROPE - GLA_GATE_RANK), w_in.dtype)
    w_in_r = jnp.concatenate([q_lat, kv_lat, gq, gk, gv, og, kpe, g_lr, misc_pad], axis=1).astype(BF16)

    wq = w_uq.reshape(MLA_Q_RANK, MLA_HEADS, MLA_QK)
    n_, r1, r2 = wq[..., :MLA_NOPE], wq[..., MLA_NOPE:MLA_NOPE + HALF_ROPE], wq[..., MLA_NOPE + HALF_ROPE:]
    w_uq_p = jnp.concatenate([n_, r1, r2, r2, r1], axis=-1).reshape(MLA_Q_RANK, MLA_HEADS * HEAD_PAD).astype(BF16)

    sc = MLA_QK ** -0.5 * math.log2(math.e)
    gn, g1, g2 = g_qn[:MLA_NOPE], g_qn[MLA_NOPE:MLA_NOPE + HALF_ROPE], g_qn[MLA_NOPE + HALF_ROPE:]
    z16, z64 = jnp.zeros((HALF_ROPE,), F32), jnp.zeros((MLA_NOPE,), F32)
    gq3 = jnp.stack([jnp.concatenate([gn, z16, z16, z16, z16]),
                     jnp.concatenate([z64, g1, g2, z16, z16]),
                     jnp.concatenate([z64, z16, z16, g2, g1])]) * math.sqrt(sc)
    kn, k1, k2 = g_kn[:MLA_NOPE], g_kn[MLA_NOPE:MLA_NOPE + HALF_ROPE], g_kn[MLA_NOPE + HALF_ROPE:]
    gk3 = jnp.stack([jnp.concatenate([kn, z16, z16, z16, z16]),
                     jnp.concatenate([z64, k1, k2, -k1, k2]),
                     jnp.concatenate([z64, -k2, k1, k2, k1])]) * math.sqrt(sc)
    eye = np.eye(HALF_ROPE, dtype=np.float32)
    zz = np.zeros((HALF_ROPE, HALF_ROPE), np.float32)
    pick1 = np.concatenate([eye, zz], axis=0)
    pick2 = np.concatenate([zz, eye], axis=0)
    z_n = np.zeros((MLA_ROPE, MLA_NOPE), np.float32)
    sel = jnp.asarray(np.concatenate([z_n, pick1, pick2, pick1, pick2, z_n, pick2, pick1, pick2, pick1], axis=1),
                      BF16)

    w_a2_p = jnp.zeros((LANES, GLA_K_ALL), F32).at[MLA_ROPE:MLA_ROPE + GLA_GATE_RANK].set(w_a2).astype(BF16)

    w_out_mla = w_out[:MLA_HEADS * MLA_V].astype(BF16)
    return {
        "score_bound": 1.01 * sc * MLA_QK * jnp.max(jnp.abs(g_qn)) * jnp.max(jnp.abs(g_kn)),
        "g_norm1": g_norm1.reshape(1, -1), "w_in": w_in_r, "g_qa": g_qa.reshape(1, -1), "w_uq": w_uq_p,
        "gq3": gq3, "g_kva": g_kva.reshape(1, -1), "w_ukv": w_ukv.astype(BF16), "sel": sel, "gk3": gk3,
        "w_uv_t": w_ukv.reshape(MLA_KV_RANK, MLA_HEADS, MLA_NOPE + MLA_V)[:, :, MLA_NOPE:]
                  .reshape(MLA_KV_RANK, MLA_HEADS * MLA_V).T.astype(BF16),
        "w_a2": w_a2_p, "b_a2": b_a2.reshape(1, -1), "g_gla": g_gla.reshape(1, -1),
        "w_out_mla": w_out_mla, "w_out_gla": w_out[MLA_HEADS * MLA_V:].astype(BF16),
        "g_norm2": g_norm2.reshape(1, -1), "w_up": w_up.astype(BF16), "w_conv": w_conv,
        "b_conv": b_conv.reshape(1, -1), "w_down": w_down.astype(BF16),
    }


def _rope_tables(n):
    inv = 1.0 / (ROPE_THETA ** (np.arange(HALF_ROPE, dtype=np.float64) / HALF_ROPE))
    ang = np.arange(n, dtype=np.float64)[:, None] * np.tile(inv, LANES // HALF_ROPE)[None, :]
    return jnp.asarray(np.cos(ang), F32), jnp.asarray(np.sin(ang), F32)


def _state_to_t(s):
    b = s.shape[0]
    return s.reshape(b, GLA_K_ALL, GLA_DV).transpose(0, 2, 1)


def _state_from_t(st):
    b = st.shape[0]
    return st.transpose(0, 2, 1).reshape(b, GLA_HEADS, GLA_DK, GLA_DV)


def _pick_tile(n, pref):
    t = min(n, pref)
    while n % t:
        t //= 2
    return t


def _layer(x, mods, w, cos_t, sin_t, cache_ckv, cache_kpe, state, hist):
    b, l, _ = x.shape
    tm = _pick_tile(l, 512)
    pos0 = 0 if cache_ckv is None else cache_ckv.shape[1]
    ckv, kpe, q, gq, gk, gv, lg, og = _inproj(x, mods, w, cos_t, sin_t, _pick_tile(l, 1024), pos0)
    if cache_ckv is None:
        k, vt = _keys(ckv, kpe, w, cos_t, sin_t, _pick_tile(l, 1024), True)
        o_mla = _attn_prompt(q, k, vt, tm, w["score_bound"])
    else:
        n_valid = pos0 + l
        lk = -(-n_valid // LANES) * LANES
        ckv_all = jnp.concatenate([cache_ckv, ckv, jnp.zeros((b, lk - n_valid, MLA_KV_RANK), F32)], axis=1)
        kpe_all = jnp.concatenate([cache_kpe, kpe, jnp.zeros((b, lk - n_valid, MLA_ROPE), F32)], axis=1)
        k, v = _keys(ckv_all, kpe_all, w, cos_t, sin_t, lk, False)
        o_mla = _attn_sample(q, k, v, n_valid)
    o_gla, st = _gla(gq, gk, gv, lg, og, _state_to_t(state), w, tm, math.gcd(b, GLA_SEQS))
    y, new_hist = _ffn(x, o_mla, o_gla, mods, hist, w, tm)
    return y, ckv, kpe, _state_from_t(st), new_hist


def kernel(x_prompt, x_sample, c_prompt, c_sample, cache_ckv, cache_kpe, state_gla, state_ffn_conv, w_ada, b_ada, g_norm1, w_in, g_qa, w_uq, g_qn, g_kva, w_ukv, g_kn, w_a2, b_a2, g_gla, w_out, g_norm2, w_up, w_conv, b_conv, w_down):
    depth = w_ada.shape[0]
    bp, lp, _ = x_prompt.shape
    bs, ls, _ = x_sample.shape
    n_pos = max(lp, cache_ckv.shape[2] + ls)
    n_pos = -(-n_pos // LANES) * LANES
    cos_t, sin_t = _rope_tables(n_pos)
    yp, ys = x_prompt, x_sample
    outs = [[] for _ in range(8)]
    for i in range(depth):
        w = _prep_weights(w_in[i], g_norm1[i], g_qa[i], w_uq[i], g_qn[i], g_kva[i], w_ukv[i], g_kn[i],
                          w_a2[i], b_a2[i], g_gla[i], w_out[i], g_norm2[i], w_up[i], w_conv[i], b_conv[i],
                          w_down[i])
        mods = _adaln(jnp.concatenate([c_prompt, c_sample], axis=0), w_ada[i], b_ada[i])
        mods = mods.reshape(bp + bs, 6, D_MODEL)
        zero_state = jnp.zeros((bp, GLA_HEADS, GLA_DK, GLA_DV), F32)
        zero_hist = jnp.zeros((bp, CONV_W - 1, FFN_DIM), F32)
        yp, a, b_, s, h = _layer(yp, mods[:bp], w, cos_t, sin_t, None, None, zero_state, zero_hist)
        for lst, val in zip(outs[:4], (a, b_, s, h)):
            lst.append(val)
        ys, a, b_, s, h = _layer(ys, mods[bp:], w, cos_t, sin_t, cache_ckv[i], cache_kpe[i],
                                 state_gla[i], state_ffn_conv[i])
        for lst, val in zip(outs[4:], (a, b_, s, h)):
            lst.append(val)
    return (yp, ys) + tuple(jnp.stack(o) for o in outs)
```

```python
import functools
import math

import jax
import jax.numpy as jnp
import numpy as np
from jax import lax
from jax.experimental import pallas as pl
from jax.experimental.pallas import tpu as pltpu

F32 = jnp.float32
BF16 = jnp.bfloat16
QK_DTYPE = jnp.bfloat16

D_MODEL = 1024
CHUNK = 64
CHUNK_SHIFT = 6
MLA_HEADS = 8
MLA_NOPE = 64
MLA_ROPE = 32
MLA_QK = MLA_NOPE + MLA_ROPE
MLA_V = 64
MLA_Q_RANK = 384
MLA_KV_RANK = 256
ROPE_THETA = 10000.0
GLA_HEADS = 4
GLA_DK = 64
GLA_DV = 128
GLA_GATE_RANK = 16
GLA_TAU = 16.0
FFN_DIM = 2816
CONV_W = 3
EPS = 1e-6

LANES = 128
HEAD_PAD = LANES
HALF_ROPE = MLA_ROPE // 2
GLA_K_ALL = GLA_HEADS * GLA_DK
GLA_V_ALL = GLA_HEADS * GLA_DV
C_QLAT = 0
C_KV = C_QLAT + MLA_Q_RANK
C_GQ = C_KV + MLA_KV_RANK
C_GK = C_GQ + GLA_K_ALL
C_GV = C_GK + GLA_K_ALL
C_OG = C_GV + GLA_V_ALL
C_MISC = C_OG + GLA_V_ALL
IN_COLS_PAD = C_MISC + LANES
FFN_TILE = 256
GLA_SEQS = 4
EXP_CLAMP = 80.0
VMEM_LIMIT = 56 * 1024 * 1024


def _const_spec(shape):
    nd = len(shape)
    return pl.BlockSpec(shape, lambda *_: (0,) * nd, pipeline_mode=pl.Buffered(1))


def _params(*sem):
    return pltpu.CompilerParams(dimension_semantics=sem, vmem_limit_bytes=VMEM_LIMIT)


def _rms_rows(x):
    return x * lax.rsqrt(jnp.mean(x * x, axis=-1, keepdims=True) + EPS)


def _ada_kernel(c_ref, w_ref, b_ref, o_ref):
    c = c_ref[...]
    s = c * jax.nn.sigmoid(c)
    o_ref[...] = jnp.dot(s, w_ref[...], precision=lax.Precision.HIGHEST,
                         preferred_element_type=F32) + b_ref[...]


def _adaln(c, w_ada, b_ada):
    n, d = c.shape
    cols = w_ada.shape[1]
    tn = 1024
    return pl.pallas_call(
        _ada_kernel,
        grid=(cols // tn,),
        in_specs=[pl.BlockSpec((n, d), lambda j: (0, 0)),
                  pl.BlockSpec((d, tn), lambda j: (0, j)),
                  pl.BlockSpec((1, tn), lambda j: (0, j))],
        out_specs=pl.BlockSpec((n, tn), lambda j: (0, j)),
        out_shape=jax.ShapeDtypeStruct((n, cols), F32),
        compiler_params=_params("arbitrary"),
        name="adaln",
    )(c, w_ada, b_ada.reshape(1, cols))


def _inproj_kernel(x_ref, mods_ref, g1_ref, win_ref, gqa_ref, wuq_ref, gq3_ref, cos_ref, sin_ref,
                   gkva_ref, wa2_ref, ba2_ref,
                   ckv_ref, kpe_ref, q_ref, gq_ref, gk_ref, gv_ref, lg_ref, og_ref, *, n_sub):
    shift = mods_ref[0, 0:1, :]
    scale = mods_ref[0, 1:2, :]
    lane = lax.broadcasted_iota(jnp.int32, (1, HEAD_PAD), 1)
    real = (lane < MLA_QK).astype(F32)
    sub = x_ref.shape[1] // n_sub

    def rows_block(rs):
        x = x_ref[0, rs, :]
        h = _rms_rows(x) * g1_ref[...] * (1.0 + scale) + shift
        p = jnp.dot(h.astype(BF16), win_ref[...], preferred_element_type=F32)

        ckv_ref[0, rs, :] = _rms_rows(p[:, C_KV:C_GQ]) * gkva_ref[...]
        misc = p[:, C_MISC:IN_COLS_PAD]
        kpe_ref[0, rs, :] = misc[:, 0:MLA_ROPE]

        qa = _rms_rows(p[:, C_QLAT:C_KV]) * gqa_ref[...]
        qu = jnp.dot(qa.astype(BF16), wuq_ref[...], preferred_element_type=F32)
        tq = gq3_ref[0:1, :] + cos_ref[rs, :] * gq3_ref[1:2, :] + sin_ref[rs, :] * gq3_ref[2:3, :]
        for hd in range(MLA_HEADS):
            xh = qu[:, hd * HEAD_PAD:(hd + 1) * HEAD_PAD]
            ss = jnp.sum(xh * xh * real, axis=-1, keepdims=True)
            r = lax.rsqrt(ss * (1.0 / MLA_QK) + EPS)
            q_ref[0, rs, hd * HEAD_PAD:(hd + 1) * HEAD_PAD] = (xh * r * tq).astype(QK_DTYPE)

        gq_ref[0, rs, :] = (p[:, C_GQ:C_GK] * (GLA_DK ** -0.5)).astype(BF16)
        gk_ref[0, rs, :] = p[:, C_GK:C_GV].astype(BF16)
        gv_ref[0, rs, :] = p[:, C_GV:C_OG].astype(BF16)
        og_ref[0, rs, :] = p[:, C_OG:C_MISC].astype(BF16)
        z = jnp.dot(misc.astype(BF16), wa2_ref[...], preferred_element_type=F32) + ba2_ref[...]
        log_sig = jnp.minimum(z, 0.0) - jnp.log1p(jnp.exp(-jnp.abs(z)))
        lg_ref[0, rs, :] = log_sig * (1.0 / GLA_TAU)

    for s in range(n_sub):
        rows_block(slice(s * sub, (s + 1) * sub))


def _inproj(x, mods, w, cos_t, sin_t, tm, pos0):
    b, l, d = x.shape
    nt = l // tm
    p0 = pos0 // tm
    row = lambda bi, li: (bi, li, 0)
    tab = lambda bi, li: (li + p0, 0)
    outs = [(MLA_KV_RANK, F32), (MLA_ROPE, F32), (MLA_HEADS * HEAD_PAD, QK_DTYPE), (GLA_K_ALL, BF16),
            (GLA_K_ALL, BF16), (GLA_V_ALL, BF16), (GLA_K_ALL, F32), (GLA_V_ALL, BF16)]
    return pl.pallas_call(
        functools.partial(_inproj_kernel, n_sub=max(1, tm // 256)),
        grid=(b, nt),
        in_specs=[pl.BlockSpec((1, tm, d), row),
                  pl.BlockSpec((1, 6, d), lambda bi, li: (bi, 0, 0)),
                  _const_spec((1, d)),
                  _const_spec((d, IN_COLS_PAD)),
                  _const_spec((1, MLA_Q_RANK)),
                  _const_spec((MLA_Q_RANK, MLA_HEADS * HEAD_PAD)),
                  _const_spec((3, HEAD_PAD)),
                  pl.BlockSpec((tm, HEAD_PAD), tab),
                  pl.BlockSpec((tm, HEAD_PAD), tab),
                  _const_spec((1, MLA_KV_RANK)),
                  _const_spec((LANES, GLA_K_ALL)),
                  _const_spec((1, GLA_K_ALL))],
        out_specs=[pl.BlockSpec((1, tm, n), row) for n, _ in outs],
        out_shape=[jax.ShapeDtypeStruct((b, l, n), dt) for n, dt in outs],
        compiler_params=_params("arbitrary", "arbitrary"),
        name="inproj",
    )(x, mods, w["g_norm1"], w["w_in"], w["g_qa"], w["w_uq"], w["gq3"], cos_t, sin_t,
      w["g_kva"], w["w_a2"], w["b_a2"])


def _keys_kernel(ckv_ref, kpe_ref, wukv_ref, wuvt_ref, sel_ref, gk3_ref, cos_ref, sin_ref, k_ref, v_ref, *,
                 transpose_v):
    ckv = ckv_ref[0].astype(BF16)
    kv = jnp.dot(ckv, wukv_ref[...], preferred_element_type=F32)
    if transpose_v:
        vt_all = lax.dot_general(wuvt_ref[...], ckv, _NT, preferred_element_type=F32)
        ones_rows = jnp.ones((MLA_V, ckv.shape[0]), BF16)
    kpe = kpe_ref[0]
    kpe_hi = kpe.astype(BF16)
    kpe_lo = (kpe - kpe_hi.astype(F32)).astype(BF16)
    uv = (jnp.dot(kpe_hi, sel_ref[...], preferred_element_type=F32)
          + jnp.dot(kpe_lo, sel_ref[...], preferred_element_type=F32))
    rot = (uv[:, :HEAD_PAD] * (cos_ref[...] * gk3_ref[1:2, :])
           + uv[:, HEAD_PAD:] * (sin_ref[...] * gk3_ref[2:3, :]))
    sp = jnp.sum(kpe * kpe, axis=-1, keepdims=True)
    lane = lax.broadcasted_iota(jnp.int32, (1, HEAD_PAD), 1)
    is_nope = lane < MLA_NOPE
    g_nope = gk3_ref[0:1, :]
    for hd in range(MLA_HEADS):
        blk = kv[:, hd * HEAD_PAD:(hd + 1) * HEAD_PAD]
        kn = jnp.where(is_nope, blk, 0.0)
        ss = jnp.sum(kn * kn, axis=-1, keepdims=True) + sp
        r = lax.rsqrt(ss * (1.0 / MLA_QK) + EPS)
        k_ref[0, :, hd * HEAD_PAD:(hd + 1) * HEAD_PAD] = ((kn * g_nope + rot) * r).astype(QK_DTYPE)
        if transpose_v:
            v_at = hd * HEAD_PAD + (hd % 2) * MLA_V
            ones_at = hd * HEAD_PAD + (1 - hd % 2) * MLA_V
            v_ref[0, ones_at:ones_at + MLA_V, :] = ones_rows
            v_ref[0, v_at:v_at + MLA_V, :] = vt_all[hd * MLA_V:(hd + 1) * MLA_V, :].astype(BF16)
        else:
            v_ref[0, :, hd * HEAD_PAD:(hd + 1) * HEAD_PAD] = jnp.where(is_nope, 1.0, blk).astype(BF16)


def _keys(ckv, kpe, w, cos_t, sin_t, tm, transpose_v):
    b, l, _ = ckv.shape
    row = lambda bi, li: (bi, li, 0)
    tab = lambda bi, li: (li, 0)
    hw = MLA_HEADS * HEAD_PAD
    if transpose_v:
        v_spec = pl.BlockSpec((1, hw, tm), lambda bi, li: (bi, 0, li))
        v_shape = jax.ShapeDtypeStruct((b, hw, l), BF16)
    else:
        v_spec = pl.BlockSpec((1, tm, hw), row)
        v_shape = jax.ShapeDtypeStruct((b, l, hw), BF16)
    return pl.pallas_call(
        functools.partial(_keys_kernel, transpose_v=transpose_v),
        grid=(b, l // tm),
        in_specs=[pl.BlockSpec((1, tm, MLA_KV_RANK), row),
                  pl.BlockSpec((1, tm, MLA_ROPE), row),
                  _const_spec((MLA_KV_RANK, hw)),
                  _const_spec((MLA_HEADS * MLA_V, MLA_KV_RANK)),
                  _const_spec((MLA_ROPE, 2 * HEAD_PAD)),
                  _const_spec((3, HEAD_PAD)),
                  pl.BlockSpec((tm, HEAD_PAD), tab),
                  pl.BlockSpec((tm, HEAD_PAD), tab)],
        out_specs=[pl.BlockSpec((1, tm, hw), row), v_spec],
        out_shape=[jax.ShapeDtypeStruct((b, l, hw), QK_DTYPE), v_shape],
        compiler_params=_params("arbitrary", "arbitrary"),
        name="keys",
    )(ckv, kpe, w["w_ukv"], w["w_uv_t"], w["sel"], w["gk3"], cos_t, sin_t)


_NT = (((1,), (1,)), ((), ()))
_TN = (((0,), (0,)), ((), ()))
MASKED = -1e30
EXP2_SAFE = 100.0


def _store_normalised(tiles, o_ref):
    odd = pl.program_id(1) % 2

    for half in range(2):
        @pl.when(odd == half)
        def _():
            lanes = slice(half * MLA_V, (half + 1) * MLA_V)
            for acc_ref, rows in tiles:
                acc = acc_ref[...]
                den = acc[(1 - half) * MLA_V:(1 - half) * MLA_V + 1, :]
                o_ref[0, rows, lanes] = (acc / den).T.astype(BF16)[:, lanes]


def _finalize_tile(acc_scr, o_ref, i, tq):
    _store_normalised([(acc_scr, pl.ds(pl.multiple_of(i * tq, tq), tq))], o_ref)


def _chunk_visible(tq):
    kc = lax.broadcasted_iota(jnp.int32, (tq, tq), 0) >> CHUNK_SHIFT
    qc = lax.broadcasted_iota(jnp.int32, (tq, tq), 1) >> CHUNK_SHIFT
    return kc <= qc


ATTN_LAG = 2
ATTN_BIG = 4
ATTN_UNROLL = 6


def _attn_bounded_kernel(iq_tab, ia_tab, j_tab, d_tab, q_ref, k_ref, vt_ref, mask_ref, o_ref, p_buf, acc_scr,
                         *, tq, nq, n_iter):
    p_buf[...] = jnp.zeros_like(p_buf)
    acc_scr[...] = jnp.zeros_like(acc_scr)
    n_slots = ATTN_LAG + 1

    def sub_step(t, slot):
        tp = jnp.maximum(t - ATTN_LAG, 0)
        jp = j_tab[tp]
        vt = vt_ref[0, :, pl.ds(pl.multiple_of(jp * tq, tq), tq)]
        cols = pl.ds(pl.multiple_of(ia_tab[tp] * tq, tq), tq)
        acc_scr[:, cols] = (jnp.where(jp == 0, 0.0, acc_scr[:, cols])
                            + jnp.dot(vt, p_buf[(slot + 1) % n_slots], preferred_element_type=F32))
        kb = k_ref[0, pl.ds(pl.multiple_of(j_tab[t] * tq, tq), tq), :]
        qt = q_ref[0, pl.ds(pl.multiple_of(iq_tab[t] * tq, tq), tq), :]
        p = jnp.exp2(lax.dot_general(kb, qt, _NT, preferred_element_type=F32)).astype(BF16)
        p_buf[slot] = p * mask_ref[d_tab[t]]

    big = ATTN_BIG * ATTN_UNROLL

    def body(it, carry):
        for u in range(big):
            sub_step(big * it + u, u % n_slots)
        return carry

    n_big, n_tail = divmod(n_iter, ATTN_BIG)
    lax.fori_loop(0, n_big, body, 0)
    for u in range(n_tail * ATTN_UNROLL):
        sub_step(jnp.int32(n_big * big + u), u % n_slots)
    _store_normalised([(acc_scr.at[:, i * tq:(i + 1) * tq], slice(i * tq, (i + 1) * tq)) for i in range(nq)],
                      o_ref)


def _attn_online_kernel(i_tab, j_tab, q_ref, k_ref, vt_ref, o_ref, s_buf, p_buf, a_buf, m_scr, acc_scr,
                        *, tq, n_steps, n_iter):
    s_buf[...] = jnp.zeros_like(s_buf)
    p_buf[...] = jnp.zeros_like(p_buf)
    a_buf[...] = jnp.zeros_like(a_buf)
    m_scr[...] = jnp.zeros_like(m_scr)
    acc_scr[...] = jnp.zeros_like(acc_scr)

    def sub_step(t, slot):
        other = 1 - slot
        t2 = jnp.maximum(t - 2, 0)
        i2, j2 = i_tab[t2], j_tab[t2]
        vt = vt_ref[0, :, pl.ds(pl.multiple_of(j2 * tq, tq), tq)]
        acc_scr[...] = a_buf[slot] * acc_scr[...] + jnp.dot(vt, p_buf[slot], preferred_element_type=F32)
        t1 = jnp.maximum(t - 1, 0)
        s = s_buf[other]
        m_old = jnp.where(j_tab[t1] == 0, -jnp.inf, m_scr[...])
        m_new = jnp.maximum(m_old, jnp.max(s, axis=0, keepdims=True))
        a_buf[other] = jnp.exp2(m_old - m_new)
        p_buf[other] = jnp.exp2(s - m_new).astype(BF16)
        m_scr[...] = m_new
        t0 = jnp.minimum(t, n_steps - 1)
        i0, j0 = i_tab[t0], j_tab[t0]
        kb = k_ref[0, pl.ds(pl.multiple_of(j0 * tq, tq), tq), :]
        qt = q_ref[0, pl.ds(pl.multiple_of(i0 * tq, tq), tq), :]
        s_buf[slot] = lax.dot_general(kb, qt, _NT, preferred_element_type=F32)

        @pl.when(i0 == j0)
        def _():
            s_buf[slot] = jnp.where(_chunk_visible(tq), s_buf[slot], MASKED)

        @pl.when(jnp.logical_and(jnp.logical_and(t >= 2, t - 2 < n_steps), i2 == j2))
        def _():
            _finalize_tile(acc_scr, o_ref, i2, tq)

    def body(it, carry):
        sub_step(2 * it, 0)
        sub_step(2 * it + 1, 1)
        return carry

    lax.fori_loop(0, n_iter, body, 0)


def _attn_specs(b, l, n_tabs, extra_in=()):
    qmap = lambda bi, hi, *_: (bi, 0, hi)
    return dict(
        num_scalar_prefetch=n_tabs,
        grid=(b, MLA_HEADS),
        in_specs=[pl.BlockSpec((1, l, HEAD_PAD), qmap),
                  pl.BlockSpec((1, l, HEAD_PAD), qmap),
                  pl.BlockSpec((1, HEAD_PAD, l), lambda bi, hi, *_: (bi, hi, 0)), *extra_in],
        out_specs=pl.BlockSpec((1, l, 2 * MLA_V), lambda bi, hi, *_: (bi, 0, hi // 2)))


def _attn_online_call(q, k, vt, tq):
    b, l, hw = q.shape
    nq = l // tq
    pairs = [(i, j) for i in range(nq) for j in range(i + 1)]
    n_steps = len(pairs)
    n_iter = (n_steps + 3) // 2
    pairs = pairs + [pairs[-1]] * (2 * n_iter - n_steps)
    i_tab = jnp.asarray([p_[0] for p_ in pairs], jnp.int32)
    j_tab = jnp.asarray([p_[1] for p_ in pairs], jnp.int32)
    grid_spec = pltpu.PrefetchScalarGridSpec(
        **_attn_specs(b, l, 2),
        scratch_shapes=[pltpu.VMEM((2, tq, tq), F32), pltpu.VMEM((2, tq, tq), BF16),
                        pltpu.VMEM((2, 1, tq), F32), pltpu.VMEM((1, tq), F32),
                        pltpu.VMEM((HEAD_PAD, tq), F32)])
    return pl.pallas_call(
        functools.partial(_attn_online_kernel, tq=tq, n_steps=n_steps, n_iter=n_iter),
        grid_spec=grid_spec,
        out_shape=jax.ShapeDtypeStruct((b, l, MLA_HEADS * MLA_V), BF16),
        compiler_params=_params("arbitrary", "arbitrary"),
        name="attn_online",
    )(i_tab, j_tab, q, k, vt)


def _attn_bounded_call(q, k, vt, tq):
    b, l, hw = q.shape
    nq = l // tq
    pairs = [(i, j) for i in range(nq) for j in range(i + 1)]
    n_iter = -(-(len(pairs) + ATTN_LAG) // ATTN_UNROLL)
    n_dummy = n_iter * ATTN_UNROLL - len(pairs)
    iq_tab = jnp.asarray([p_[0] for p_ in pairs] + [nq - 1] * n_dummy, jnp.int32)
    ia_tab = jnp.asarray([p_[0] for p_ in pairs] + [nq] * n_dummy, jnp.int32)
    j_tab = jnp.asarray([p_[1] for p_ in pairs] + [0] * n_dummy, jnp.int32)
    d_tab = jnp.asarray([int(p_[0] == p_[1]) for p_ in pairs] + [0] * n_dummy, jnp.int32)
    chunk_of = np.arange(tq) // CHUNK
    masks = jnp.asarray(np.stack([np.ones((tq, tq), np.float32),
                                  (chunk_of[:, None] <= chunk_of[None, :]).astype(np.float32)]), BF16)
    grid_spec = pltpu.PrefetchScalarGridSpec(
        **_attn_specs(b, l, 4, [pl.BlockSpec((2, tq, tq), lambda *_: (0, 0, 0), pipeline_mode=pl.Buffered(1))]),
        scratch_shapes=[pltpu.VMEM((ATTN_LAG + 1, tq, tq), BF16), pltpu.VMEM((HEAD_PAD, (nq + 1) * tq), F32)])
    return pl.pallas_call(
        functools.partial(_attn_bounded_kernel, tq=tq, nq=nq, n_iter=n_iter),
        grid_spec=grid_spec,
        out_shape=jax.ShapeDtypeStruct((b, l, MLA_HEADS * MLA_V), BF16),
        compiler_params=_params("arbitrary", "arbitrary"),
        name="attn_bounded",
    )(iq_tab, ia_tab, j_tab, d_tab, q, k, vt, masks)


def _attn_prompt(q, k, vt, tq, score_bound):
    return lax.cond(score_bound <= EXP2_SAFE,
                    functools.partial(_attn_bounded_call, tq=tq),
                    functools.partial(_attn_online_call, tq=tq),
                    q, k, vt)


def _attn_sample_kernel(q_ref, k_ref, v_ref, o_ref, *, n_valid):
    outs = []
    for hd in range(MLA_HEADS):
        cols = slice(hd * HEAD_PAD, (hd + 1) * HEAD_PAD)
        s = lax.dot_general(q_ref[0, :, cols], k_ref[0, :, cols], _NT, preferred_element_type=F32)
        col = lax.broadcasted_iota(jnp.int32, s.shape, 1)
        s = jnp.where(col < n_valid, s, MASKED)
        p = jnp.exp2(s - jnp.max(s, axis=-1, keepdims=True))
        acc = jnp.dot(p.astype(BF16), v_ref[0, :, cols], preferred_element_type=F32)
        outs.append(acc / acc[:, 0:1])
    lane = lax.broadcasted_iota(jnp.int32, (1, HEAD_PAD), 1)
    for pair in range(MLA_HEADS // 2):
        even = pltpu.roll(outs[2 * pair], MLA_V, 1)
        o_ref[0, :, pair * HEAD_PAD:(pair + 1) * HEAD_PAD] = (
            jnp.where(lane < MLA_V, even, outs[2 * pair + 1]).astype(BF16))


def _attn_sample(q, k, v, n_valid):
    b, t, hw = q.shape
    lk = k.shape[1]
    bmap = lambda bi: (bi, 0, 0)
    return pl.pallas_call(
        functools.partial(_attn_sample_kernel, n_valid=n_valid),
        grid=(b,),
        in_specs=[pl.BlockSpec((1, t, hw), bmap),
                  pl.BlockSpec((1, lk, hw), bmap),
                  pl.BlockSpec((1, lk, hw), bmap)],
        out_specs=pl.BlockSpec((1, t, MLA_HEADS * MLA_V), bmap),
        out_shape=jax.ShapeDtypeStruct((b, t, MLA_HEADS * MLA_V), BF16),
        compiler_params=_params("arbitrary"),
        name="attn_sample",
    )(q, k, v)


def _gla_kernel(gq_ref, gk_ref, gv_ref, lg_ref, og_ref, tri_ref, ggla_ref, st0_ref,
                o_ref, stf_ref, st_scr, b_scr, *, tg, nb):
    li = pl.program_id(1)

    @pl.when(li == 0)
    def _():
        st_scr[...] = st0_ref[...]

    tri = tri_ref[...]
    tb = tri.shape[0]
    for bb in range(nb):
        for r0 in range(0, tg, tb):
            lg = lg_ref[bb, r0:r0 + tb, :]
            lg_hi = lg.astype(BF16)
            lg_lo = (lg - lg_hi.astype(F32)).astype(BF16)
            b_scr[bb, r0:r0 + tb, :] = (jnp.dot(tri, lg_hi, preferred_element_type=F32)
                                        + jnp.dot(tri, lg_lo, preferred_element_type=F32))

    lane = lax.broadcasted_iota(jnp.int32, (1, GLA_K_ALL), 1)
    head_of_lane = lane >> CHUNK_SHIFT
    ri = lax.broadcasted_iota(jnp.int32, (GLA_HEADS * CHUNK, CHUNK), 0) & (CHUNK - 1)
    cj = lax.broadcasted_iota(jnp.int32, (GLA_HEADS * CHUNK, CHUNK), 1)
    causal = cj <= ri
    g_out = ggla_ref[...]

    def chunk(bb, r0):
        b = b_scr[bb, r0:r0 + CHUNK, :]
        b_mid = b[CHUNK // 2 - 1:CHUNK // 2, :]
        b_last = b[CHUNK - 1:CHUNK, :]
        q = gq_ref[bb, r0:r0 + CHUNK, :].astype(F32)
        k = gk_ref[bb, r0:r0 + CHUNK, :].astype(F32)
        qe = q * jnp.exp(jnp.minimum(b - b_mid, EXP_CLAMP))
        ke = (k * jnp.exp(jnp.minimum(b_mid - b, EXP_CLAMP))).astype(BF16)
        qb = q * jnp.exp(b)
        kd = k * jnp.exp(b_last - b)
        qe, qb = qe.astype(BF16), qb.astype(BF16)
        zero = jnp.zeros_like(qe)
        qe_st = jnp.concatenate([jnp.where(head_of_lane == hd, qe, zero) for hd in range(GLA_HEADS)], axis=0)
        qb_st = jnp.concatenate([jnp.where(head_of_lane == hd, qb, zero) for hd in range(GLA_HEADS)], axis=0)
        a_st = lax.dot_general(qe_st, ke, _NT, preferred_element_type=F32)
        a_st = jnp.where(causal, a_st, 0.0).astype(BF16)
        st = st_scr[bb]
        o_inter = lax.dot_general(qb_st, st.astype(BF16), _NT, preferred_element_type=F32)
        vk = lax.dot_general(gv_ref[bb, r0:r0 + CHUNK, :], kd.astype(BF16), _TN,
                             preferred_element_type=F32)
        upd = vk[0:GLA_DV, :]
        for hd in range(1, GLA_HEADS):
            upd = jnp.where(head_of_lane == hd, vk[hd * GLA_DV:(hd + 1) * GLA_DV, :], upd)
        for hd in range(GLA_HEADS):
            vh = gv_ref[bb, r0:r0 + CHUNK, hd * GLA_DV:(hd + 1) * GLA_DV]
            o = (o_inter[hd * CHUNK:(hd + 1) * CHUNK, :]
                 + jnp.dot(a_st[hd * CHUNK:(hd + 1) * CHUNK, :], vh, preferred_element_type=F32))
            og = og_ref[bb, r0:r0 + CHUNK, hd * GLA_DV:(hd + 1) * GLA_DV].astype(F32)
            o = _rms_rows(o) * g_out * (og * jax.nn.sigmoid(og))
            o_ref[bb, r0:r0 + CHUNK, hd * GLA_DV:(hd + 1) * GLA_DV] = o.astype(BF16)
        st_scr[bb] = st * jnp.exp(b_last) + upd

    for c in range(tg // CHUNK):
        for bb in range(nb):
            chunk(bb, c * CHUNK)

    stf_ref[...] = st_scr[...]


def _gla(gq, gk, gv, lg, og, st0, w, tg, nb):
    b, l, _ = gq.shape
    row = lambda bi, li: (bi, li, 0)
    fix = lambda bi, li: (bi, 0, 0)
    tb = min(tg, 256)
    pos = np.arange(tb)
    tri = jnp.asarray(((pos[:, None] // CHUNK == pos[None, :] // CHUNK) & (pos[None, :] <= pos[:, None]))
                      .astype(np.float32), BF16)
    return pl.pallas_call(
        functools.partial(_gla_kernel, tg=tg, nb=nb),
        grid=(b // nb, l // tg),
        in_specs=[pl.BlockSpec((nb, tg, GLA_K_ALL), row),
                  pl.BlockSpec((nb, tg, GLA_K_ALL), row),
                  pl.BlockSpec((nb, tg, GLA_V_ALL), row),
                  pl.BlockSpec((nb, tg, GLA_K_ALL), row),
                  pl.BlockSpec((nb, tg, GLA_V_ALL), row),
                  _const_spec((tb, tb)),
                  _const_spec((1, GLA_DV)),
                  pl.BlockSpec((nb, GLA_DV, GLA_K_ALL), fix)],
        out_specs=[pl.BlockSpec((nb, tg, GLA_V_ALL), row),
                   pl.BlockSpec((nb, GLA_DV, GLA_K_ALL), fix)],
        out_shape=[jax.ShapeDtypeStruct((b, l, GLA_V_ALL), BF16),
                   jax.ShapeDtypeStruct((b, GLA_DV, GLA_K_ALL), F32)],
        scratch_shapes=[pltpu.VMEM((nb, GLA_DV, GLA_K_ALL), F32), pltpu.VMEM((nb, tg, GLA_K_ALL), F32)],
        compiler_params=_params("arbitrary", "arbitrary"),
        name="gla",
    )(gq, gk, gv, lg, og, tri, w["g_gla"], st0)


def _ffn_kernel(x_ref, om_ref, ogl_ref, mods_ref, g2_ref, wom_ref, wog_ref, wup_ref, wcv_ref, bcv_ref,
                wdn_ref, hist_ref, y_ref, nh_ref, carry_scr, a_scr, act_scr, *, tm):
    li = pl.program_id(1)
    hist_rows = CONV_W - 1
    pad = 8

    @pl.when(li == 0)
    def _():
        carry_scr[...] = jnp.zeros_like(carry_scr)
        carry_scr[pad - hist_rows:pad, :] = hist_ref[0]

    mixed = (jnp.dot(om_ref[0], wom_ref[...], preferred_element_type=F32)
             + jnp.dot(ogl_ref[0], wog_ref[...], preferred_element_type=F32))
    x1 = x_ref[0] + mods_ref[0, 2:3, :] * mixed
    h = (_rms_rows(x1) * g2_ref[...] * (1.0 + mods_ref[0, 4:5, :]) + mods_ref[0, 3:4, :]).astype(BF16)

    for f in range(FFN_DIM // FFN_TILE):
        c0 = f * FFN_TILE
        a = jnp.dot(h, wup_ref[:, c0:c0 + FFN_TILE], preferred_element_type=F32)
        g = jnp.dot(h, wup_ref[:, FFN_DIM + c0:FFN_DIM + c0 + FFN_TILE], preferred_element_type=F32)
        a_scr[0:pad, :] = carry_scr[:, c0:c0 + FFN_TILE]
        a_scr[pad:pad + tm, :] = a
        carry_scr[:, c0:c0 + FFN_TILE] = a[tm - pad:tm, :]
        nh_ref[0, :, c0:c0 + FFN_TILE] = a[tm - hist_rows:tm, :]
        conv = (bcv_ref[:, c0:c0 + FFN_TILE]
                + wcv_ref[2:3, c0:c0 + FFN_TILE] * a
                + wcv_ref[1:2, c0:c0 + FFN_TILE] * a_scr[pad - 1:pad - 1 + tm, :]
                + wcv_ref[0:1, c0:c0 + FFN_TILE] * a_scr[pad - 2:pad - 2 + tm, :])
        act = (jax.nn.gelu(conv) * g).astype(BF16)
        act_scr[:, c0:c0 + FFN_TILE] = act

    y = jnp.dot(act_scr[...], wdn_ref[...], preferred_element_type=F32)
    y_ref[0] = x1 + mods_ref[0, 5:6, :] * y


def _ffn(x, o_mla, o_gla, mods, hist, w, tm):
    b, l, d = x.shape
    row = lambda bi, li: (bi, li, 0)
    fix = lambda bi, li: (bi, 0, 0)
    hw = MLA_HEADS * MLA_V
    return pl.pallas_call(
        functools.partial(_ffn_kernel, tm=tm),
        grid=(b, l // tm),
        in_specs=[pl.BlockSpec((1, tm, d), row),
                  pl.BlockSpec((1, tm, hw), row),
                  pl.BlockSpec((1, tm, GLA_V_ALL), row),
                  pl.BlockSpec((1, 6, d), fix),
                  _const_spec((1, d)),
                  _const_spec((hw, d)),
                  _const_spec((GLA_V_ALL, d)),
                  _const_spec((d, 2 * FFN_DIM)),
                  _const_spec((CONV_W, FFN_DIM)),
                  _const_spec((1, FFN_DIM)),
                  _const_spec((FFN_DIM, d)),
                  pl.BlockSpec((1, CONV_W - 1, FFN_DIM), fix)],
        out_specs=[pl.BlockSpec((1, tm, d), row),
                   pl.BlockSpec((1, CONV_W - 1, FFN_DIM), fix)],
        out_shape=[jax.ShapeDtypeStruct((b, l, d), F32),
                   jax.ShapeDtypeStruct((b, CONV_W - 1, FFN_DIM), F32)],
        scratch_shapes=[pltpu.VMEM((8, FFN_DIM), F32),
                        pltpu.VMEM((tm + 8, FFN_TILE), F32),
                        pltpu.VMEM((tm, FFN_DIM), BF16)],
        compiler_params=_params("arbitrary", "arbitrary"),
        name="ffn",
    )(x, o_mla, o_gla, mods, w["g_norm2"], w["w_out_mla"], w["w_out_gla"], w["w_up"], w["w_conv"],
      w["b_conv"], w["w_down"], hist)


def _prep_weights(w_in, g_norm1, g_qa, w_uq, g_qn, g_kva, w_ukv, g_kn, w_a2, b_a2, g_gla, w_out,
                  g_norm2, w_up, w_conv, b_conv, w_down):
    o, cols = 0, []
    for n in (MLA_Q_RANK, MLA_KV_RANK, MLA_ROPE, GLA_K_ALL, GLA_K_ALL, GLA_V_ALL, GLA_GATE_RANK, GLA_V_ALL):
        cols.append(w_in[:, o:o + n])
        o += n
    q_lat, kv_lat, kpe, gq, gk, gv, g_lr, og = cols
    misc_pad = jnp.zeros((D_MODEL, LANES - MLA_ROPE - GLA_GATE_RANK), w_in.dtype)
    w_in_r = jnp.concatenate([q_lat, kv_lat, gq, gk, gv, og, kpe, g_lr, misc_pad], axis=1).astype(BF16)

    wq = w_uq.reshape(MLA_Q_RANK, MLA_HEADS, MLA_QK)
    n_, r1, r2 = wq[..., :MLA_NOPE], wq[..., MLA_NOPE:MLA_NOPE + HALF_ROPE], wq[..., MLA_NOPE + HALF_ROPE:]
    w_uq_p = jnp.concatenate([n_, r1, r2, r2, r1], axis=-1).reshape(MLA_Q_RANK, MLA_HEADS * HEAD_PAD).astype(BF16)

    sc = MLA_QK ** -0.5 * math.log2(math.e)
    gn, g1, g2 = g_qn[:MLA_NOPE], g_qn[MLA_NOPE:MLA_NOPE + HALF_ROPE], g_qn[MLA_NOPE + HALF_ROPE:]
    z16, z64 = jnp.zeros((HALF_ROPE,), F32), jnp.zeros((MLA_NOPE,), F32)
    gq3 = jnp.stack([jnp.concatenate([gn, z16, z16, z16, z16]),
                     jnp.concatenate([z64, g1, g2, z16, z16]),
                     jnp.concatenate([z64, z16, z16, g2, g1])]) * math.sqrt(sc)
    kn, k1, k2 = g_kn[:MLA_NOPE], g_kn[MLA_NOPE:MLA_NOPE + HALF_ROPE], g_kn[MLA_NOPE + HALF_ROPE:]
    gk3 = jnp.stack([jnp.concatenate([kn, z16, z16, z16, z16]),
                     jnp.concatenate([z64, k1, k2, -k1, k2]),
                     jnp.concatenate([z64, -k2, k1, k2, k1])]) * math.sqrt(sc)
    eye = np.eye(HALF_ROPE, dtype=np.float32)
    zz = np.zeros((HALF_ROPE, HALF_ROPE), np.float32)
    pick1 = np.concatenate([eye, zz], axis=0)
    pick2 = np.concatenate([zz, eye], axis=0)
    z_n = np.zeros((MLA_ROPE, MLA_NOPE), np.float32)
    sel = jnp.asarray(np.concatenate([z_n, pick1, pick2, pick1, pick2, z_n, pick2, pick1, pick2, pick1], axis=1),
                      BF16)

    w_a2_p = jnp.zeros((LANES, GLA_K_ALL), F32).at[MLA_ROPE:MLA_ROPE + GLA_GATE_RANK].set(w_a2).astype(BF16)

    w_out_mla = w_out[:MLA_HEADS * MLA_V].astype(BF16)
    return {
        "score_bound": 1.01 * sc * MLA_QK * jnp.max(jnp.abs(g_qn)) * jnp.max(jnp.abs(g_kn)),
        "g_norm1": g_norm1.reshape(1, -1), "w_in": w_in_r, "g_qa": g_qa.reshape(1, -1), "w_uq": w_uq_p,
        "gq3": gq3, "g_kva": g_kva.reshape(1, -1), "w_ukv": w_ukv.astype(BF16), "sel": sel, "gk3": gk3,
        "w_uv_t": w_ukv.reshape(MLA_KV_RANK, MLA_HEADS, MLA_NOPE + MLA_V)[:, :, MLA_NOPE:]
                  .reshape(MLA_KV_RANK, MLA_HEADS * MLA_V).T.astype(BF16),
        "w_a2": w_a2_p, "b_a2": b_a2.reshape(1, -1), "g_gla": g_gla.reshape(1, -1),
        "w_out_mla": w_out_mla, "w_out_gla": w_out[MLA_HEADS * MLA_V:].astype(BF16),
        "g_norm2": g_norm2.reshape(1, -1), "w_up": w_up.astype(BF16), "w_conv": w_conv,
        "b_conv": b_conv.reshape(1, -1), "w_down": w_down.astype(BF16),
    }


def _rope_tables(n):
    inv = 1.0 / (ROPE_THETA ** (np.arange(HALF_ROPE, dtype=np.float64) / HALF_ROPE))
    ang = np.arange(n, dtype=np.float64)[:, None] * np.tile(inv, LANES // HALF_ROPE)[None, :]
    return jnp.asarray(np.cos(ang), F32), jnp.asarray(np.sin(ang), F32)


def _state_to_t(s):
    b = s.shape[0]
    return s.reshape(b, GLA_K_ALL, GLA_DV).transpose(0, 2, 1)


def _state_from_t(st):
    b = st.shape[0]
    return st.transpose(0, 2, 1).reshape(b, GLA_HEADS, GLA_DK, GLA_DV)


def _pick_tile(n, pref):
    t = min(n, pref)
    while n % t:
        t //= 2
    return t


def _layer(x, mods, w, cos_t, sin_t, cache_ckv, cache_kpe, state, hist):
    b, l, _ = x.shape
    tm = _pick_tile(l, 512)
    pos0 = 0 if cache_ckv is None else cache_ckv.shape[1]
    ckv, kpe, q, gq, gk, gv, lg, og = _inproj(x, mods, w, cos_t, sin_t, _pick_tile(l, 1024), pos0)
    if cache_ckv is None:
        k, vt = _keys(ckv, kpe, w, cos_t, sin_t, _pick_tile(l, 1024), True)
        o_mla = _attn_prompt(q, k, vt, tm, w["score_bound"])
    else:
        n_valid = pos0 + l
        lk = -(-n_valid // LANES) * LANES
        ckv_all = jnp.concatenate([cache_ckv, ckv, jnp.zeros((b, lk - n_valid, MLA_KV_RANK), F32)], axis=1)
        kpe_all = jnp.concatenate([cache_kpe, kpe, jnp.zeros((b, lk - n_valid, MLA_ROPE), F32)], axis=1)
        k, v = _keys(ckv_all, kpe_all, w, cos_t, sin_t, lk, False)
        o_mla = _attn_sample(q, k, v, n_valid)
    o_gla, st = _gla(gq, gk, gv, lg, og, _state_to_t(state), w, tm, math.gcd(b, GLA_SEQS))
    y, new_hist = _ffn(x, o_mla, o_gla, mods, hist, w, tm)
    return y, ckv, kpe, _state_from_t(st), new_hist


def kernel(x_prompt, x_sample, c_prompt, c_sample, cache_ckv, cache_kpe, state_gla, state_ffn_conv, w_ada, b_ada, g_norm1, w_in, g_qa, w_uq, g_qn, g_kva, w_ukv, g_kn, w_a2, b_a2, g_gla, w_out, g_norm2, w_up, w_conv, b_conv, w_down):
    depth = w_ada.shape[0]
    bp, lp, _ = x_prompt.shape
    bs, ls, _ = x_sample.shape
    n_pos = max(lp, cache_ckv.shape[2] + ls)
    n_pos = -(-n_pos // LANES) * LANES
    cos_t, sin_t = _rope_tables(n_pos)
    yp, ys = x_prompt, x_sample
    outs = [[] for _ in range(8)]
    for i in range(depth):
        w = _prep_weights(w_in[i], g_norm1[i], g_qa[i], w_uq[i], g_qn[i], g_kva[i], w_ukv[i], g_kn[i],
                          w_a2[i], b_a2[i], g_gla[i], w_out[i], g_norm2[i], w_up[i], w_conv[i], b_conv[i],
                          w_down[i])
        mods = _adaln(jnp.concatenate([c_prompt, c_sample], axis=0), w_ada[i], b_ada[i])
        mods = mods.reshape(bp + bs, 6, D_MODEL)
        zero_state = jnp.zeros((bp, GLA_HEADS, GLA_DK, GLA_DV), F32)
        zero_hist = jnp.zeros((bp, CONV_W - 1, FFN_DIM), F32)
        yp, a, b_, s, h = _layer(yp, mods[:bp], w, cos_t, sin_t, None, None, zero_state, zero_hist)
        for lst, val in zip(outs[:4], (a, b_, s, h)):
            lst.append(val)
        ys, a, b_, s, h = _layer(ys, mods[bp:], w, cos_t, sin_t, cache_ckv[i], cache_kpe[i],
                                 state_gla[i], state_ffn_conv[i])
        for lst, val in zip(outs[4:], (a, b_, s, h)):
            lst.append(val)
    return (yp, ys) + tuple(jnp.stack(o) for o in outs)
```
